```python
import math
import jax
import jax.numpy as jnp
from jax import lax
import numpy as np

D_MODEL = 1024
BATCH = 4
SEQ = 8192
DEPTH = 2

GRID_W = 64
CTX_LEN = 256
N_MIXERS = 2
EPS = 1e-6
N_MOD = 6
ATT_HEADS = 16
ATT_KV_HEADS = 4
HEAD_DIM = 64
ATT_Q_W = ATT_HEADS * HEAD_DIM
ATT_KV_W = ATT_KV_HEADS * HEAD_DIM
ATT_PROJ_W = ATT_Q_W + 2 * ATT_KV_W
WINDOW = 128
ATT_BLOCK = 128
ROPE_BASE = 10000.0
ROPE_FREQS = HEAD_DIM // 4
DN_HEADS = 8
DN_DK = 128
DN_DV = 128
DN_KW = DN_HEADS * DN_DK
DN_VW = DN_HEADS * DN_DV
DN_QKV_W = 2 * DN_KW + DN_VW
DN_PROJ_W = DN_QKV_W + DN_VW + 4 * DN_HEADS
CONV_K = 5
DN_CHUNK = 64
N_GROUPS = 4
EXPERTS_PER_GROUP = 8
N_EXPERTS = N_GROUPS * EXPERTS_PER_GROUP
TOP_K = 2
EXPERT_FF = 512
MOE_BLOCK = 512
N_ATTN_LAYERS = (DEPTH + 1) // 2
N_DN_LAYERS = DEPTH // 2
F32 = jnp.float32

kernel_name = 'hybrid_swa_deltanet_hmoe_dit'


def rms_norm(x, gain):
    xf = x.astype(F32)
    y = xf * lax.rsqrt(jnp.mean(xf * xf, axis=-1, keepdims=True) + EPS) * gain.astype(F32)
    return y.astype(x.dtype)


def l2_normalize(x):
    return x * lax.rsqrt(jnp.sum(x * x, axis=-1, keepdims=True) + EPS)


def axial_rope_tables(n):
    rows = n // GRID_W
    row = jnp.repeat(jnp.arange(rows, dtype=F32), GRID_W)
    col = jnp.tile(jnp.arange(GRID_W, dtype=F32), rows)
    inv = ROPE_BASE ** (-jnp.arange(ROPE_FREQS, dtype=F32) / ROPE_FREQS)
    ang = jnp.stack([row[:, None] * inv, col[:, None] * inv], axis=1)
    return jnp.cos(ang), jnp.sin(ang)


def apply_axial_rope(x, cos, sin):
    b, t, hh, dh = x.shape
    xr = x.astype(F32).reshape(b, t, hh, 2, 2, ROPE_FREQS)
    x0, x1 = xr[..., 0, :], xr[..., 1, :]
    cs, sn = cos[None, :, None], sin[None, :, None]
    out = jnp.stack([x0 * cs - x1 * sn, x1 * cs + x0 * sn], axis=-2)
    return out.reshape(b, t, hh, dh).astype(x.dtype)


def head_rms(x, gain):
    xf = x.astype(F32)
    return (xf * lax.rsqrt(jnp.mean(xf * xf, axis=-1, keepdims=True) + EPS) * gain.astype(F32)).astype(x.dtype)


def window_attention(q, k, v, kc, vc, sink):
    b, n, h, dh = q.shape
    g = k.shape[2]
    r = h // g
    nb = n // ATT_BLOCK
    scale = dh ** -0.5
    qb = q.reshape(b, nb, ATT_BLOCK, g, r, dh).transpose(1, 0, 2, 3, 4, 5)

    def band(t):
        tp = jnp.pad(t, ((0, 0), (ATT_BLOCK, ATT_BLOCK), (0, 0), (0, 0))).reshape(b, nb + 2, ATT_BLOCK, g, dh)
        w = jnp.concatenate([tp[:, :-2], tp[:, 1:-1], tp[:, 2:]], axis=2)
        return w.transpose(1, 0, 2, 3, 4)

    kw, vw = band(k), band(v)
    qi = jnp.arange(ATT_BLOCK)
    kj = jnp.arange(3 * ATT_BLOCK)
    rel = kj[None, :] - ATT_BLOCK - qi[:, None]
    kabs = (jnp.arange(nb)[:, None] - 1) * ATT_BLOCK + kj[None, :]
    mask = (jnp.abs(rel) <= WINDOW)[None] & ((kabs >= 0) & (kabs < n))[:, None, :]
    sk = sink.astype(F32).reshape(1, g, r, 1, 1)
    kc32, vc32 = kc.astype(F32), vc.astype(F32)

    def one_block(args):
        qblk, kblk, vblk, mblk = args
        s_w = jnp.einsum('bqgrd,bkgd->bgrqk', qblk, kblk, preferred_element_type=F32) * scale
        s_w = jnp.where(mblk, s_w, -jnp.inf)
        s_c = jnp.einsum('bqgrd,bkgd->bgrqk', qblk.astype(F32), kc32) * scale
        m = jnp.maximum(jnp.maximum(s_w.max(-1, keepdims=True), s_c.max(-1, keepdims=True)), sk)
        p_w = jnp.exp(s_w - m)
        p_c = jnp.exp(s_c - m)
        den = p_w.sum(-1) + p_c.sum(-1) + jnp.exp(sk - m)[..., 0]
        o = jnp.einsum('bgrqk,bkgd->bqgrd', p_w, vblk.astype(F32)) + jnp.einsum('bgrqk,bkgd->bqgrd', p_c, vc32)
        return (o / den.transpose(0, 3, 1, 2)[..., None]).astype(qblk.dtype)

    o = lax.map(one_block, (qb, kw, vw, mask))
    return o.transpose(1, 0, 2, 3, 4, 5).reshape(b, n, h * dh)


def context_attention(qc, kc, vc, sink):
    b, l, h, dh = qc.shape
    g = kc.shape[2]
    r = h // g
    qg = qc.reshape(b, l, g, r, dh)
    s = jnp.einsum('bqgrd,bkgd->bgrqk', qg, kc, preferred_element_type=F32) * dh ** -0.5
    sk = sink.astype(F32).reshape(1, g, r, 1, 1)
    m = jnp.maximum(s.max(-1, keepdims=True), sk)
    p = jnp.exp(s - m)
    den = p.sum(-1) + jnp.exp(sk - m)[..., 0]
    o = jnp.einsum('bgrqk,bkgd->bqgrd', p, vc.astype(F32)) / den.transpose(0, 3, 1, 2)[..., None]
    return o.reshape(b, l, h * dh).astype(qc.dtype)


def attention_mixer(h_lat, h_ctx, w_qkv, w_o, q_gain, k_gain, sink, cos, sin, ctx_out):
    def project(h):
        bsz, t_len, _ = h.shape
        p = h @ w_qkv
        q = p[..., :ATT_Q_W].reshape(bsz, t_len, ATT_HEADS, HEAD_DIM)
        k = p[..., ATT_Q_W:ATT_Q_W + ATT_KV_W].reshape(bsz, t_len, ATT_KV_HEADS, HEAD_DIM)
        v = p[..., ATT_Q_W + ATT_KV_W:].reshape(bsz, t_len, ATT_KV_HEADS, HEAD_DIM)
        return head_rms(q, q_gain), head_rms(k, k_gain), v

    ql, kl, vl = project(h_lat)
    ql, kl = apply_axial_rope(ql, cos, sin), apply_axial_rope(kl, cos, sin)
    qc, kc, vc = project(h_ctx)
    y_lat = window_attention(ql, kl, vl, kc, vc, sink) @ w_o
    y_ctx = context_attention(qc, kc, vc, sink) @ w_o if ctx_out else None
    return y_lat, y_ctx


def centred_depthwise_conv(x, w):
    ch = x.shape[-1]
    return lax.conv_general_dilated(x, w[:, None, :].astype(x.dtype), window_strides=(1,),
                                    padding=[(CONV_K // 2, CONV_K // 2)],
                                    dimension_numbers=('NWC', 'WIO', 'NWC'), feature_group_count=ch)


def gated_delta_chunked(q, k, v, g, beta, s0, with_out):
    b, t_len, h, dk = k.shape
    dv = v.shape[-1]
    nc = t_len // DN_CHUNK

    def chunks(t):
        return jnp.moveaxis(t.reshape(b, nc, DN_CHUNK, h, *t.shape[3:]), 3, 1)

    q = chunks(q) * dk ** -0.5
    k, v, g, beta = chunks(k), chunks(v), chunks(g), chunks(beta)
    gcum = jnp.cumsum(g, axis=-1)
    tril = jnp.tril(jnp.ones((DN_CHUNK, DN_CHUNK), bool))
    stril = jnp.tril(jnp.ones((DN_CHUNK, DN_CHUNK), bool), -1)
    decay = jnp.exp(jnp.where(tril, gcum[..., :, None] - gcum[..., None, :], -jnp.inf))
    kb = k * beta[..., None]
    lmat = jnp.where(stril, jnp.einsum('bhncd,bhnsd->bhncs', kb, k) * decay, 0.0)
    a_mat = lmat + jnp.eye(DN_CHUNK, dtype=F32)
    rhs = jnp.concatenate([v * beta[..., None], kb * jnp.exp(gcum)[..., None]], axis=-1)
    sol = lax.linalg.triangular_solve(a_mat, rhs, left_side=True, lower=True, unit_diagonal=True)
    u, w = sol[..., :dv], sol[..., dv:]
    glast = gcum[..., -1:]
    kd = k * jnp.exp(glast - gcum)[..., None]
    eg = jnp.exp(glast)[..., None]
    if with_out:
        qg = q * jnp.exp(gcum)[..., None]
        qk = jnp.where(tril, jnp.einsum('bhncd,bhnsd->bhncs', q, k) * decay, 0.0)
        xs = tuple(jnp.moveaxis(t, 2, 0) for t in (u, w, kd, eg, qg, qk))

        def step(s, xs_i):
            u_i, w_i, kd_i, eg_i, qg_i, qk_i = xs_i
            v_new = u_i - jnp.einsum('bhck,bhkv->bhcv', w_i, s)
            o = jnp.einsum('bhck,bhkv->bhcv', qg_i, s) + jnp.einsum('bhcs,bhsv->bhcv', qk_i, v_new)
            return s * eg_i + jnp.einsum('bhck,bhcv->bhkv', kd_i, v_new), o

        s_fin, o = lax.scan(step, s0, xs)
        return o.transpose(1, 0, 3, 2, 4).reshape(b, t_len, h, dv), s_fin
    xs = tuple(jnp.moveaxis(t, 2, 0) for t in (u, w, kd, eg))

    def state_step(s, xs_i):
        u_i, w_i, kd_i, eg_i = xs_i
        v_new = u_i - jnp.einsum('bhck,bhkv->bhcv', w_i, s)
        return s * eg_i + jnp.einsum('bhck,bhcv->bhkv', kd_i, v_new), None

    s_fin, _ = lax.scan(state_step, s0, xs)
    return None, s_fin


def gated_out(o, z, o_gain, w_o):
    b, t_len = o.shape[:2]
    zf = z.astype(F32).reshape(b, t_len, DN_HEADS, DN_DV)
    y = o * lax.rsqrt(jnp.mean(o * o, axis=-1, keepdims=True) + EPS) * o_gain.astype(F32) * jax.nn.silu(zf)
    return y.reshape(b, t_len, DN_VW).astype(z.dtype) @ w_o


def deltanet_mixer(h_lat, h_ctx, w_in, conv_w, a_log, dt_bias, o_gain, w_o, ctx_out):
    def project(h):
        bsz, t_len, _ = h.shape
        p = h @ w_in
        qkv = jax.nn.silu(centred_depthwise_conv(p[..., :DN_QKV_W], conv_w)).astype(F32)
        z = p[..., DN_QKV_W:DN_QKV_W + DN_VW]
        ab = p[..., DN_QKV_W + DN_VW:].astype(F32).reshape(bsz, t_len, 2, 2, DN_HEADS)
        q = l2_normalize(qkv[..., :DN_KW].reshape(bsz, t_len, DN_HEADS, DN_DK))
        k = l2_normalize(qkv[..., DN_KW:2 * DN_KW].reshape(bsz, t_len, DN_HEADS, DN_DK))
        v = qkv[..., 2 * DN_KW:].reshape(bsz, t_len, DN_HEADS, DN_DV)
        g = -jnp.exp(a_log.astype(F32)) * jax.nn.softplus(ab[:, :, 0] + dt_bias.astype(F32))
        beta = jax.nn.sigmoid(ab[:, :, 1])
        return q, k, v, g, beta, z

    qc, kc, vc, gc, bc, zc = project(h_ctx)
    ql, kl, vl, gl, bl, zl = project(h_lat)
    s0 = jnp.zeros((h_lat.shape[0], DN_HEADS, DN_DK, DN_DV), F32)
    rev = lambda t: jnp.flip(t, axis=1)
    oc_f, sc_f = gated_delta_chunked(qc, kc, vc, gc[:, :, 0], bc[:, :, 0], s0, ctx_out)
    ol_f, _ = gated_delta_chunked(ql, kl, vl, gl[:, :, 0], bl[:, :, 0], sc_f, True)
    oc_b, sc_b = gated_delta_chunked(rev(qc), rev(kc), rev(vc), rev(gc[:, :, 1]), rev(bc[:, :, 1]), s0, ctx_out)
    ol_b, _ = gated_delta_chunked(rev(ql), rev(kl), rev(vl), rev(gl[:, :, 1]), rev(bl[:, :, 1]), sc_b, True)
    y_lat = gated_out(ol_f + rev(ol_b), zl, o_gain, w_o)
    y_ctx = gated_out(oc_f + rev(oc_b), zc, o_gain, w_o) if ctx_out else None
    return y_lat, y_ctx


def hier_moe(h, w_group, b_group, w_expert, b_expert, w_in, w_out):
    n_tok, d = h.shape
    hf = h.astype(F32)
    p_group = jax.nn.softmax(hf @ w_group.astype(F32) + b_group.astype(F32), axis=-1)
    p_top, g_idx = lax.top_k(p_group, 1)
    le = (hf @ w_expert.astype(F32) + b_expert.astype(F32)).reshape(n_tok, N_GROUPS, EXPERTS_PER_GROUP)
    le_g = jnp.take_along_axis(le, g_idx[:, :, None], axis=1)[:, 0]
    e_val, e_idx = lax.top_k(le_g, TOP_K)
    gate = (jax.nn.softmax(e_val, axis=-1) * p_top).reshape(-1)
    eid = (g_idx * EXPERTS_PER_GROUP + e_idx).reshape(-1)
    n_slot = n_tok * TOP_K
    tok = jnp.arange(n_slot, dtype=jnp.int32) // TOP_K
    order = jnp.argsort(eid)
    e_sorted = eid[order]
    counts = jnp.bincount(eid, length=N_EXPERTS)
    padded = (counts + MOE_BLOCK - 1) // MOE_BLOCK * MOE_BLOCK
    ends_pad = jnp.cumsum(padded)
    start_pad = ends_pad - padded
    start = jnp.cumsum(counts) - counts
    dest = start_pad[e_sorted] + jnp.arange(n_slot, dtype=jnp.int32) - start[e_sorted]
    n_blocks = (n_slot + N_EXPERTS * (MOE_BLOCK - 1)) // MOE_BLOCK
    n_buf = n_blocks * MOE_BLOCK
    buf_tok = jnp.full((n_buf,), n_tok, jnp.int32).at[dest].set(tok[order])
    buf_gate = jnp.zeros((n_buf,), F32).at[dest].set(gate[order])
    blk_e = jnp.minimum(jnp.searchsorted(ends_pad, jnp.arange(n_blocks) * MOE_BLOCK, side='right'), N_EXPERTS - 1)
    h_pad = jnp.concatenate([h, jnp.zeros((1, d), h.dtype)], axis=0)
    xb = h_pad[buf_tok].reshape(n_blocks, MOE_BLOCK, d)

    def expert_block(args):
        xblk, e = args
        hu = xblk @ w_in[e]
        return (jax.nn.silu(hu[:, :EXPERT_FF]) * hu[:, EXPERT_FF:]) @ w_out[e]

    yb = lax.map(expert_block, (xb, blk_e)).reshape(n_buf, d)
    y = jnp.zeros((n_tok + 1, d), F32).at[buf_tok].add(yb.astype(F32) * buf_gate[:, None])
    return y[:n_tok].astype(h.dtype)


def setup_inputs(seed: int = 0) -> dict:
    key = jax.random.key(seed)
    ks = jax.random.split(key, 24)
    D = D_MODEL

    def nrm(k, shape, s):
        return jax.random.normal(k, shape, F32) * s

    dt = jnp.exp(jax.random.uniform(ks[15], (N_DN_LAYERS, 2, DN_HEADS), F32, math.log(1e-3), math.log(1e-1)))
    return {
        'x': nrm(ks[0], (BATCH, SEQ, D), 1.0),
        'c': nrm(ks[1], (BATCH, D), 1.0),
        'ctx': nrm(ks[2], (BATCH, CTX_LEN, D), 1.0),
        'c_ctx': nrm(ks[3], (D,), 1.0),
        'ada_w': nrm(ks[4], (DEPTH, D, N_MOD * D), 0.5 * D ** -0.5),
        'ada_b': nrm(ks[5], (DEPTH, N_MOD * D), 0.02),
        'norm_g': 1.0 + nrm(ks[6], (DEPTH, 2, D), 0.02),
        'attn_w_qkv': nrm(ks[7], (N_ATTN_LAYERS, D, ATT_PROJ_W), D ** -0.5),
        'attn_w_o': nrm(ks[8], (N_ATTN_LAYERS, ATT_Q_W, D), ATT_Q_W ** -0.5),
        'attn_q_gain': 1.0 + nrm(ks[9], (N_ATTN_LAYERS, HEAD_DIM), 0.02),
        'attn_k_gain': 1.0 + nrm(ks[10], (N_ATTN_LAYERS, HEAD_DIM), 0.02),
        'attn_sink': nrm(ks[11], (N_ATTN_LAYERS, ATT_HEADS), 0.5),
        'dn_w_in': nrm(ks[12], (N_DN_LAYERS, D, DN_PROJ_W), D ** -0.5),
        'dn_conv_w': nrm(ks[13], (N_DN_LAYERS, CONV_K, DN_QKV_W), CONV_K ** -0.5),
        'dn_a_log': jnp.log(jax.random.uniform(ks[14], (N_DN_LAYERS, 2, DN_HEADS), F32, 1.0, 16.0)),
        'dn_dt_bias': dt + jnp.log(-jnp.expm1(-dt)),
        'dn_o_gain': 1.0 + nrm(ks[16], (N_DN_LAYERS, DN_DV), 0.02),
        'dn_w_o': nrm(ks[17], (N_DN_LAYERS, DN_VW, D), DN_VW ** -0.5),
        'moe_w_group': nrm(ks[18], (DEPTH, D, N_GROUPS), D ** -0.5),
        'moe_b_group': nrm(ks[19], (DEPTH, N_GROUPS), 0.01),
        'moe_w_expert': nrm(ks[20], (DEPTH, D, N_EXPERTS), D ** -0.5),
        'moe_b_expert': nrm(ks[21], (DEPTH, N_EXPERTS), 0.01),
        'moe_w_in': nrm(ks[22], (DEPTH, N_EXPERTS, D, 2 * EXPERT_FF), D ** -0.5),
        'moe_w_out': nrm(ks[23], (DEPTH, N_EXPERTS, EXPERT_FF, D), EXPERT_FF ** -0.5),
    }


def reference(x, c, ctx, c_ctx, ada_w, ada_b, norm_g, attn_w_qkv, attn_w_o, attn_q_gain, attn_k_gain,
              attn_sink, dn_w_in, dn_conv_w, dn_a_log, dn_dt_bias, dn_o_gain, dn_w_o, moe_w_group,
              moe_b_group, moe_w_expert, moe_b_expert, moe_w_in, moe_w_out):
    b, n, d = x.shape
    l = ctx.shape[1]
    cos, sin = axial_rope_tables(n)
    s_lat = jax.nn.silu(c.astype(F32))
    s_ctx = jax.nn.silu(c_ctx.astype(F32))
    h_lat, h_ctx = x, ctx
    for i in range(DEPTH):
        last = i == DEPTH - 1
        j = i // N_MIXERS
        mod_l = (s_lat @ ada_w[i].astype(F32) + ada_b[i].astype(F32)).reshape(b, 1, N_MOD, d).astype(x.dtype)
        mod_c = (s_ctx @ ada_w[i].astype(F32) + ada_b[i].astype(F32)).reshape(N_MOD, d).astype(x.dtype)
        sh1_l, sc1_l, g1_l, sh2_l, sc2_l, g2_l = [mod_l[:, :, m] for m in range(N_MOD)]
        sh1_c, sc1_c, g1_c, sh2_c, sc2_c, g2_c = [mod_c[m] for m in range(N_MOD)]
        a_l = rms_norm(h_lat, norm_g[i, 0]) * (1 + sc1_l) + sh1_l
        a_c = rms_norm(h_ctx, norm_g[i, 0]) * (1 + sc1_c) + sh1_c
        if i % N_MIXERS == 0:
            y_l, y_c = attention_mixer(a_l, a_c, attn_w_qkv[j], attn_w_o[j], attn_q_gain[j], attn_k_gain[j],
                                       attn_sink[j], cos, sin, not last)
        else:
            y_l, y_c = deltanet_mixer(a_l, a_c, dn_w_in[j], dn_conv_w[j], dn_a_log[j], dn_dt_bias[j],
                                      dn_o_gain[j], dn_w_o[j], not last)
        h_lat = h_lat + g1_l * y_l
        m_l = rms_norm(h_lat, norm_g[i, 1]) * (1 + sc2_l) + sh2_l
        if last:
            y = hier_moe(m_l.reshape(-1, d), moe_w_group[i], moe_b_group[i], moe_w_expert[i], moe_b_expert[i],
                         moe_w_in[i], moe_w_out[i])
            h_lat = h_lat + g2_l * y.reshape(b, n, d)
        else:
            h_ctx = h_ctx + g1_c * y_c
            m_c = rms_norm(h_ctx, norm_g[i, 1]) * (1 + sc2_c) + sh2_c
            toks = jnp.concatenate([m_l.reshape(-1, d), m_c.reshape(-1, d)], axis=0)
            y = hier_moe(toks, moe_w_group[i], moe_b_group[i], moe_w_expert[i], moe_b_expert[i],
                         moe_w_in[i], moe_w_out[i])
            h_lat = h_lat + g2_l * y[:b * n].reshape(b, n, d)
            h_ctx = h_ctx + g2_c * y[b * n:].reshape(b, l, d)
    return h_lat
```

```python
import functools

import jax
import jax.numpy as jnp
from jax import lax
from jax.experimental import pallas as pl
from jax.experimental.pallas import tpu as pltpu

F32 = jnp.float32
BF16 = jnp.bfloat16

EPS = 1e-6
N_MOD = 6
GRID_W = 64
ATT_HEADS = 16
ATT_KV_HEADS = 4
HEAD_DIM = 64
ATT_BLOCK = 128
WINDOW = 128
ROPE_BASE = 10000.0
ROPE_FREQS = HEAD_DIM // 4
DN_HEADS = 8
DN_DK = 128
DN_DV = 128
CONV_K = 5
DN_CHUNK = 64
N_GROUPS = 4
EXPERTS_PER_GROUP = 8
N_EXPERTS = N_GROUPS * EXPERTS_PER_GROUP
TOP_K = 2
EXPERT_FF = 512

LANES = 128
SUBLANES = 8
VMEM_LIMIT_BYTES = 56 * 1024 * 1024

ROW_TILE = 512
DN_ROW_TILE = 256
MOE_TILE = 512
ROUTER_LANES = 128


def _cparams(*sem):
    return pltpu.CompilerParams(dimension_semantics=sem, vmem_limit_bytes=VMEM_LIMIT_BYTES)


def _dot(a, b):
    return jnp.dot(a, b, preferred_element_type=F32)


def _dot_nt(a, b):
    return lax.dot_general(a, b, (((1,), (1,)), ((), ())), preferred_element_type=F32)


def _split_bf16(a):
    hi = a.astype(BF16)
    lo = (a - hi.astype(F32)).astype(BF16)
    return hi, lo


def _dot3(a, b):
    ah, al = _split_bf16(a)
    bh, bl = _split_bf16(b)
    return _dot(ah, bh) + _dot(al, bh) + _dot(ah, bl)


def _silu(x):
    return x * (1.0 / (1.0 + jnp.exp(-x)))


def _rms_mod(x, gain, scale, shift):
    y = x * lax.rsqrt(jnp.mean(x * x, axis=-1, keepdims=True) + EPS) * gain
    return y * (1.0 + scale) + shift


def _ada_kernel(s_ref, w_ref, b_ref, o_ref):
    s = _silu(s_ref[...])
    o_ref[0] = _dot3(s, w_ref[0]) + b_ref[0]


def _ada_modulation(cond, ada_w, ada_b):
    depth, d, _ = ada_w.shape
    rows = cond.shape[0]
    return pl.pallas_call(
        _ada_kernel,
        out_shape=jax.ShapeDtypeStruct((depth, rows, N_MOD * d), F32),
        grid=(depth * N_MOD,),
        in_specs=[
            pl.BlockSpec((rows, d), lambda j: (0, 0)),
            pl.BlockSpec((1, d, d), lambda j: (j // N_MOD, 0, j % N_MOD)),
            pl.BlockSpec((1, 1, d), lambda j: (j // N_MOD, 0, j % N_MOD)),
        ],
        out_specs=pl.BlockSpec((1, rows, d), lambda j: (j // N_MOD, 0, j % N_MOD)),
        compiler_params=_cparams("arbitrary"),
        name="ada_modulation",
    )(cond, ada_w, ada_b.reshape(depth, 1, N_MOD * d))


def _attn_proj_kernel(h_ref, mod_ref, ng_ref, w_ref, hg_ref, gm_ref, cos_ref, sin_ref,
                      q_ref, k_ref, v_ref, *, n_q_blocks, n_kv_blocks):
    a = _rms_mod(h_ref[...], ng_ref[...], mod_ref[0, 1:2, :], mod_ref[0, 0:1, :]).astype(BF16)
    cos = cos_ref[...]
    sin = sin_ref[...]
    lane = lax.broadcasted_iota(jnp.int32, (1, LANES), 1)
    first_half = (lane % (2 * ROPE_FREQS)) < ROPE_FREQS
    for j in range(n_q_blocks + n_kv_blocks):
        is_q = j < n_q_blocks
        p = _dot(a, w_ref[:, j * LANES:(j + 1) * LANES])
        ms = _dot((p * p).astype(BF16), gm_ref[0 if is_q else 1])
        y = p * lax.rsqrt(ms + EPS) * hg_ref[(0 if is_q else 1):(1 if is_q else 2), :]
        partner = jnp.where(first_half, pltpu.roll(y, LANES - ROPE_FREQS, 1), pltpu.roll(y, ROPE_FREQS, 1))
        y = y * cos + partner * sin
        if is_q:
            q_ref[:, j * LANES:(j + 1) * LANES] = (y * (HEAD_DIM ** -0.5)).astype(BF16)
        else:
            jj = j - n_q_blocks
            k_ref[:, jj * LANES:(jj + 1) * LANES] = y.astype(BF16)
    for j in range(n_kv_blocks):
        c0 = (n_q_blocks + n_kv_blocks + j) * LANES
        v_ref[:, j * LANES:(j + 1) * LANES] = _dot(a, w_ref[:, c0:c0 + LANES]).astype(BF16)


def _rope_tables(n, pad_rows):
    rows = n // GRID_W
    row = jnp.repeat(jnp.arange(rows, dtype=F32), GRID_W)
    col = jnp.tile(jnp.arange(GRID_W, dtype=F32), rows)
    inv = ROPE_BASE ** (-jnp.arange(ROPE_FREQS, dtype=F32) / ROPE_FREQS)
    ang_r = row[:, None] * inv
    ang_c = col[:, None] * inv
    ang = jnp.concatenate([ang_r, ang_r, ang_c, ang_c], axis=1)
    sign = jnp.tile(jnp.concatenate([-jnp.ones((ROPE_FREQS,), F32), jnp.ones((ROPE_FREQS,), F32)]), 2)
    cos = jnp.tile(jnp.cos(ang), (1, 2))
    sin = jnp.tile(jnp.sin(ang) * sign, (1, 2))
    cos = jnp.concatenate([cos, jnp.ones((pad_rows, LANES), F32)], axis=0)
    sin = jnp.concatenate([sin, jnp.zeros((pad_rows, LANES), F32)], axis=0)
    return cos, sin


def _attn_projection(h, mod, norm_gain, w_qkv, q_gain, k_gain, *, bsz, n):
    t_all, d = h.shape
    tm = ROW_TILE
    tiles_per_seq = n // tm
    lat_tiles = bsz * tiles_per_seq
    q_w = ATT_HEADS * HEAD_DIM
    kv_w = ATT_KV_HEADS * HEAD_DIM
    n_q_blocks = q_w // LANES
    wq = w_qkv[:, :q_w]
    wk = w_qkv[:, q_w:q_w + kv_w].reshape(d, ATT_KV_HEADS, 1, HEAD_DIM)
    wv = w_qkv[:, q_w + kv_w:].reshape(d, ATT_KV_HEADS, 1, HEAD_DIM)
    wk = jnp.broadcast_to(wk, (d, ATT_KV_HEADS, 2, HEAD_DIM)).reshape(d, ATT_KV_HEADS * LANES)
    wv = jnp.broadcast_to(wv, (d, ATT_KV_HEADS, 2, HEAD_DIM)).reshape(d, ATT_KV_HEADS * LANES)
    w = jnp.concatenate([wq, wk, wv], axis=1).astype(BF16)
    head_gain = jnp.stack([jnp.tile(q_gain, 2), jnp.tile(k_gain, 2)]).astype(F32)
    same_head = (jnp.arange(LANES)[:, None] // HEAD_DIM) == (jnp.arange(LANES)[None, :] // HEAD_DIM)
    gm = jnp.stack([same_head.astype(F32) / HEAD_DIM, jnp.full((LANES, LANES), 1.0 / LANES, F32)]).astype(BF16)
    cos, sin = _rope_tables(n, tm)

    def rope_idx(i):
        return (jnp.where(i < lat_tiles, i % tiles_per_seq, tiles_per_seq), 0)

    kern = functools.partial(_attn_proj_kernel, n_q_blocks=n_q_blocks, n_kv_blocks=ATT_KV_HEADS)
    return pl.pallas_call(
        kern,
        out_shape=(jax.ShapeDtypeStruct((t_all, q_w), BF16),
                   jax.ShapeDtypeStruct((t_all, ATT_KV_HEADS * LANES), BF16),
                   jax.ShapeDtypeStruct((t_all, ATT_KV_HEADS * LANES), BF16)),
        grid=(t_all // tm,),
        in_specs=[
            pl.BlockSpec((tm, d), lambda i: (i, 0)),
            pl.BlockSpec((1, N_MOD, d), lambda i: (jnp.minimum(i // tiles_per_seq, bsz), 0, 0)),
            pl.BlockSpec((1, d), lambda i: (0, 0)),
            pl.BlockSpec(w.shape, lambda i: (0, 0)),
            pl.BlockSpec((2, LANES), lambda i: (0, 0)),
            pl.BlockSpec((2, LANES, LANES), lambda i: (0, 0, 0)),
            pl.BlockSpec((tm, LANES), rope_idx),
            pl.BlockSpec((tm, LANES), rope_idx),
        ],
        out_specs=(pl.BlockSpec((tm, q_w), lambda i: (i, 0)),
                   pl.BlockSpec((tm, ATT_KV_HEADS * LANES), lambda i: (i, 0)),
                   pl.BlockSpec((tm, ATT_KV_HEADS * LANES), lambda i: (i, 0))),
        compiler_params=_cparams("arbitrary"),
        name="attn_projection",
    )(h, mod, norm_gain.reshape(1, d), w, head_gain, gm, cos, sin)


def _attention_kernel(sink_ref, q_ref, kp_ref, kc_ref, kn_ref, kx_ref, vp_ref, vc_ref, vn_ref, vx_ref,
                      o_ref, *, n_lat_blocks):
    i = pl.program_id(1)
    blk = ATT_BLOCK
    rep = ATT_HEADS // ATT_KV_HEADS
    rows = rep * blk
    is_lat = i < n_lat_blocks
    r = lax.broadcasted_iota(jnp.int32, (rows, 3 * blk), 0) % blk
    c = lax.broadcasted_iota(jnp.int32, (rows, 3 * blk), 1)
    band = (c >= r) & (c <= r + 2 * WINDOW)
    lo_ok = jnp.where(i > 0, 0, blk)
    hi_ok = jnp.where(i < n_lat_blocks - 1, 3 * blk, 2 * blk)
    valid = band & (c >= lo_ok) & (c < hi_ok) & is_lat
    bias = jnp.where(valid, 0.0, -jnp.inf).astype(F32)
    lane = lax.broadcasted_iota(jnp.int32, (1, LANES), 1)
    low = lane < HEAD_DIM
    row_id = lax.broadcasted_iota(jnp.int32, (rows, 1), 0)
    for g in range(ATT_KV_HEADS):
        cols = slice(g * LANES, (g + 1) * LANES)
        parts = []
        for pb in range(rep // 2):
            qp = q_ref[:, (g * (rep // 2) + pb) * LANES:(g * (rep // 2) + pb + 1) * LANES]
            zero = jnp.zeros_like(qp)
            parts += [jnp.where(low, qp, zero), jnp.where(low, zero, qp)]
        q4 = jnp.concatenate(parts, axis=0)
        kwin = jnp.concatenate([kp_ref[:, cols], kc_ref[:, cols], kn_ref[:, cols]], axis=0)
        vwin = jnp.concatenate([vp_ref[:, cols], vc_ref[:, cols], vn_ref[:, cols]], axis=0)
        s_w = _dot_nt(q4, kwin) + bias
        s_c = _dot_nt(q4, kx_ref[:, cols])
        sk = jnp.zeros((rows, 1), F32)
        for rr in range(rep):
            sk = jnp.where(row_id // blk == rr, sink_ref[g * rep + rr], sk)
        m = jnp.maximum(jnp.maximum(jnp.max(s_w, axis=-1, keepdims=True),
                                    jnp.max(s_c, axis=-1, keepdims=True)), sk)
        p_w = jnp.exp(s_w - m)
        p_c = jnp.exp(s_c - m)
        den = jnp.sum(p_w, axis=-1, keepdims=True) + jnp.sum(p_c, axis=-1, keepdims=True) + jnp.exp(sk - m)
        o = _dot(p_w.astype(BF16), vwin) + _dot(p_c.astype(BF16), vx_ref[:, cols])
        o = o / den
        for pb in range(rep // 2):
            even = o[(2 * pb) * blk:(2 * pb + 1) * blk]
            odd = o[(2 * pb + 1) * blk:(2 * pb + 2) * blk]
            c0 = (g * (rep // 2) + pb) * LANES
            o_ref[:, c0:c0 + LANES] = jnp.where(low, even, odd).astype(o_ref.dtype)


def _attention(q, kw, vw, sink, *, bsz, n, l):
    t_all, q_w = q.shape
    blk = ATT_BLOCK
    nb = n // blk
    cb = l // blk
    lat_blocks = bsz * nb
    kvw = kw.shape[1]

    def q_idx(b, i, s):
        return (jnp.where(i < nb, b * nb + i, lat_blocks + b * cb + (i - nb)), 0)

    def win_idx(off):
        def f(b, i, s):
            j = jnp.clip(jnp.where(i < nb, i, 0) + off, 0, nb - 1)
            return (b * nb + j, 0)
        return f

    def ctx_idx(b, i, s):
        return ((bsz * n) // l + b, 0)

    kv_spec = lambda off: pl.BlockSpec((blk, kvw), win_idx(off))
    grid_spec = pltpu.PrefetchScalarGridSpec(
        num_scalar_prefetch=1,
        grid=(bsz, nb + cb),
        in_specs=[pl.BlockSpec((blk, q_w), q_idx),
                  kv_spec(-1), kv_spec(0), kv_spec(1), pl.BlockSpec((l, kvw), ctx_idx),
                  kv_spec(-1), kv_spec(0), kv_spec(1), pl.BlockSpec((l, kvw), ctx_idx)],
        out_specs=pl.BlockSpec((blk, q_w), q_idx),
    )
    return pl.pallas_call(
        functools.partial(_attention_kernel, n_lat_blocks=nb),
        out_shape=jax.ShapeDtypeStruct((t_all, q_w), BF16),
        grid_spec=grid_spec,
        compiler_params=_cparams("arbitrary", "arbitrary"),
        name="window_attention",
    )(sink.astype(F32), q, kw, kw, kw, kw, vw, vw, vw, vw)


def _route(logits, prefix_fn):
    tm = logits.shape[0]
    lane = lax.broadcasted_iota(jnp.int32, (tm, ROUTER_LANES), 1)
    neg = jnp.full_like(logits, -jnp.inf)
    big = jnp.full_like(lane, ROUTER_LANES)
    is_g = lane < N_GROUPS
    gl = jnp.where(is_g, logits, neg)
    gmax = jnp.max(gl, axis=-1, keepdims=True)
    g_idx = jnp.min(jnp.where(is_g & (gl == gmax), lane, big), axis=-1, keepdims=True)
    p_top = 1.0 / jnp.sum(jnp.exp(gl - gmax), axis=-1, keepdims=True)
    e_lo = N_GROUPS + g_idx * EXPERTS_PER_GROUP
    in_grp = (lane >= e_lo) & (lane < e_lo + EXPERTS_PER_GROUP)
    el = jnp.where(in_grp, logits, neg)
    v1 = jnp.max(el, axis=-1, keepdims=True)
    l1 = jnp.min(jnp.where(in_grp & (el == v1), lane, big), axis=-1, keepdims=True)
    el2 = jnp.where(lane == l1, neg, el)
    v2 = jnp.max(el2, axis=-1, keepdims=True)
    l2 = jnp.min(jnp.where(in_grp & (lane != l1) & (el2 == v2), lane, big), axis=-1, keepdims=True)
    ex = jnp.exp(v2 - v1)
    w1 = (1.0 / (1.0 + ex)) * p_top
    w2 = (ex / (1.0 + ex)) * p_top
    oh1 = (lane == l1)
    oh2 = (lane == l2)
    prefix = prefix_fn(oh1.astype(F32) + oh2.astype(F32))
    r1 = jnp.sum(jnp.where(oh1, prefix, 0.0), axis=-1, keepdims=True)
    r2 = jnp.sum(jnp.where(oh2, prefix, 0.0), axis=-1, keepdims=True)
    vals = [(l1 - N_GROUPS).astype(F32), (l2 - N_GROUPS).astype(F32), w1, w2, r1, r2]
    out = jnp.zeros_like(logits)
    for k, v in enumerate(vals):
        out = jnp.where(lane == k, v, out)
    return out


def _mixer_out_kernel(*refs, mode):
    if mode == "attn":
        o_ref, = refs[:1]
        rest = refs[1:]
    else:
        of_ref, ob_ref, z_ref, og_ref = refs[:4]
        rest = refs[4:]
    (wo_ref, h_ref, mod_ref, ng_ref, wr_ref, br_ref,
     h_out_ref, m_ref, route_ref, cnt_ref, carry_ref) = rest
    i = pl.program_id(0)

    @pl.when(i == 0)
    def _():
        carry_ref[...] = jnp.zeros_like(carry_ref)

    if mode == "attn":
        y_in = o_ref[...]
    else:
        parts = []
        for hd in range(DN_HEADS):
            cs = slice(hd * DN_DV, (hd + 1) * DN_DV)
            o = of_ref[:, cs] + ob_ref[:, cs]
            y = o * lax.rsqrt(jnp.mean(o * o, axis=-1, keepdims=True) + EPS) * og_ref[...] * _silu(z_ref[:, cs])
            parts.append(y.astype(BF16))
        y_in = jnp.concatenate(parts, axis=1)
    y = _dot(y_in, wo_ref[...])
    h1 = h_ref[...] + mod_ref[0, 2:3, :] * y
    h_out_ref[...] = h1
    m = _rms_mod(h1, ng_ref[...], mod_ref[0, 4:5, :], mod_ref[0, 3:4, :])
    m_ref[...] = m
    logits = _dot3(m, wr_ref[...]) + br_ref[...]
    tm = logits.shape[0]
    rr = lax.broadcasted_iota(jnp.int32, (tm, tm), 0)
    cc = lax.broadcasted_iota(jnp.int32, (tm, tm), 1)
    strict_lower = jnp.where(cc < rr, 1.0, 0.0).astype(BF16)

    def prefix_fn(cnt):
        carry = carry_ref[0:1, :]
        carry_ref[0:1, :] = carry + jnp.sum(cnt, axis=0, keepdims=True)
        return _dot(strict_lower, cnt.astype(BF16)) + carry

    route_ref[...] = _route(logits, prefix_fn)
    cnt_ref[...] = carry_ref[...]


def _mixer_output(mode, mix_inputs, w_o, h, mod, norm_gain, w_router, b_router, *, bsz, n, rows):
    d = h.shape[1]
    tm = ROW_TILE
    tiles_per_seq = n // tm
    row_spec = lambda w: pl.BlockSpec((tm, w), lambda i: (i, 0))
    const = lambda shp: pl.BlockSpec(shp, lambda i: tuple(0 for _ in shp))
    if mode == "attn":
        mix_specs = [row_spec(mix_inputs[0].shape[1])]
    else:
        of, ob, z, og = mix_inputs
        mix_specs = [row_spec(of.shape[1]), row_spec(ob.shape[1]), row_spec(z.shape[1]), const(og.shape)]
    return pl.pallas_call(
        functools.partial(_mixer_out_kernel, mode=mode),
        out_shape=(jax.ShapeDtypeStruct((rows, d), F32),
                   jax.ShapeDtypeStruct((rows, d), F32),
                   jax.ShapeDtypeStruct((rows, ROUTER_LANES), F32),
                   jax.ShapeDtypeStruct((SUBLANES, ROUTER_LANES), F32)),
        grid=(rows // tm,),
        in_specs=mix_specs + [
            const(w_o.shape),
            row_spec(d),
            pl.BlockSpec((1, N_MOD, d), lambda i: (jnp.minimum(i // tiles_per_seq, bsz), 0, 0)),
            const((1, d)),
            const(w_router.shape),
            const(b_router.shape),
        ],
        out_specs=(row_spec(d), row_spec(d), row_spec(ROUTER_LANES), const((SUBLANES, ROUTER_LANES))),
        scratch_shapes=[pltpu.VMEM((SUBLANES, ROUTER_LANES), F32)],
        compiler_params=_cparams("arbitrary"),
        name="mixer_output_" + mode,
    )(*mix_inputs, w_o, h, mod, norm_gain.reshape(1, d), w_router, b_router)


def _dispatch_kernel(dest_hbm, m_hbm, xs_in, xs_hbm, idx_smem, idx_sem, row_sem, *, tm, n_tiles):
    del xs_in
    i = pl.program_id(0)
    cp = pltpu.make_async_copy(dest_hbm.at[pl.ds(i * (2 * tm), 2 * tm)], idx_smem, idx_sem)
    cp.start()
    cp.wait()

    def row_copy(src_row, dst_row):
        return pltpu.make_async_copy(m_hbm.at[pl.ds(src_row, 1), :], xs_hbm.at[pl.ds(dst_row, 1), :], row_sem)

    def issue(r, carry):
        row_copy(i * tm + r, idx_smem[r]).start()
        row_copy(i * tm + r, idx_smem[tm + r]).start()
        return carry

    lax.fori_loop(0, tm, issue, 0)

    def drain(r, carry):
        row_copy(0, 0).wait()
        return carry

    @pl.when(i > 0)
    def _():
        lax.fori_loop(0, 2 * tm, drain, 0)

    @pl.when(i == n_tiles - 1)
    def _():
        lax.fori_loop(0, 2 * tm, drain, 0)


def _moe_dispatch(dest_tiles, m, n_buf):
    rows, d = m.shape
    tm = ROW_TILE
    n_tiles = rows // tm
    xs0 = jnp.zeros((n_buf, d), F32)
    return pl.pallas_call(
        functools.partial(_dispatch_kernel, tm=tm, n_tiles=n_tiles),
        out_shape=jax.ShapeDtypeStruct((n_buf, d), F32),
        grid=(n_tiles,),
        in_specs=[pl.BlockSpec(memory_space=pl.ANY)] * 3,
        out_specs=pl.BlockSpec(memory_space=pl.ANY),
        scratch_shapes=[pltpu.SMEM((2 * tm,), jnp.int32), pltpu.SemaphoreType.DMA, pltpu.SemaphoreType.DMA],
        input_output_aliases={2: 0},
        compiler_params=_cparams("arbitrary"),
        name="moe_dispatch",
    )(dest_tiles, m, xs0)


def _expert_kernel(blk_e_ref, nact_ref, x_ref, wi_ref, wo_ref, y_ref):
    del blk_e_ref
    i = pl.program_id(0)

    @pl.when(i < nact_ref[0])
    def _():
        x = x_ref[...].astype(BF16)
        hu = _dot(x, wi_ref[0].astype(BF16))
        act = _silu(hu[:, :EXPERT_FF]) * hu[:, EXPERT_FF:]
        y_ref[...] = _dot(act.astype(BF16), wo_ref[0].astype(BF16))

    @pl.when(i >= nact_ref[0])
    def _():
        y_ref[...] = jnp.zeros_like(y_ref)


def _moe_experts(xs, blk_e, n_active, w_in, w_out):
    n_buf, d = xs.shape
    tm = MOE_TILE
    n_blocks = n_buf // tm
    ff2 = w_in.shape[2]
    grid_spec = pltpu.PrefetchScalarGridSpec(
        num_scalar_prefetch=2,
        grid=(n_blocks,),
        in_specs=[pl.BlockSpec((tm, d), lambda i, be, na: (i, 0)),
                  pl.BlockSpec((1, d, ff2), lambda i, be, na: (be[i], 0, 0)),
                  pl.BlockSpec((1, ff2 // 2, d), lambda i, be, na: (be[i], 0, 0))],
        out_specs=pl.BlockSpec((tm, d), lambda i, be, na: (i, 0)),
    )
    return pl.pallas_call(
        _expert_kernel,
        out_shape=jax.ShapeDtypeStruct((n_buf, d), F32),
        grid_spec=grid_spec,
        compiler_params=_cparams("arbitrary"),
        name="moe_experts",
    )(blk_e, n_active, xs, w_in, w_out)


def _combine_kernel(dest_hbm, yb_hbm, route_ref, h_ref, mod_ref, o_ref,
                    idx_smem, y0_buf, y1_buf, idx_sem, row_sem, *, tm):
    i = pl.program_id(0)
    cp = pltpu.make_async_copy(dest_hbm.at[pl.ds(i * (2 * tm), 2 * tm)], idx_smem, idx_sem)
    cp.start()
    cp.wait()

    def row_copy(src_row, buf, r):
        return pltpu.make_async_copy(yb_hbm.at[pl.ds(src_row, 1), :], buf.at[pl.ds(r, 1), :], row_sem)

    def issue(r, carry):
        row_copy(idx_smem[r], y0_buf, r).start()
        row_copy(idx_smem[tm + r], y1_buf, r).start()
        return carry

    lax.fori_loop(0, tm, issue, 0)

    def drain(r, carry):
        row_copy(0, y0_buf, 0).wait()
        return carry

    lax.fori_loop(0, 2 * tm, drain, 0)
    route = route_ref[...]
    y = route[:, 2:3] * y0_buf[...] + route[:, 3:4] * y1_buf[...]
    o_ref[...] = h_ref[...] + mod_ref[0, 5:6, :] * y


def _moe_combine(dest_tiles, yb, route, h, mod, *, bsz, n):
    rows, d = h.shape
    tm = ROW_TILE
    tiles_per_seq = n // tm
    return pl.pallas_call(
        functools.partial(_combine_kernel, tm=tm),
        out_shape=jax.ShapeDtypeStruct((rows, d), F32),
        grid=(rows // tm,),
        in_specs=[pl.BlockSpec(memory_space=pl.ANY),
                  pl.BlockSpec(memory_space=pl.ANY),
                  pl.BlockSpec((tm, ROUTER_LANES), lambda i: (i, 0)),
                  pl.BlockSpec((tm, d), lambda i: (i, 0)),
                  pl.BlockSpec((1, N_MOD, d), lambda i: (jnp.minimum(i // tiles_per_seq, bsz), 0, 0))],
        out_specs=pl.BlockSpec((tm, d), lambda i: (i, 0)),
        scratch_shapes=[pltpu.SMEM((2 * tm,), jnp.int32), pltpu.VMEM((tm, d), F32), pltpu.VMEM((tm, d), F32),
                        pltpu.SemaphoreType.DMA, pltpu.SemaphoreType.DMA],
        compiler_params=_cparams("arbitrary"),
        name="moe_combine",
    )(dest_tiles, yb, route, h, mod)


def _moe_layer(h1, m, route, counts, mod, w_in, w_out, *, bsz, n):
    rows, d = m.shape
    tm_r = ROW_TILE
    tm_e = MOE_TILE
    n_slot = rows * TOP_K
    eid = route[:, 0:TOP_K].astype(jnp.int32)
    rank = route[:, 4:4 + TOP_K].astype(jnp.int32)
    cnt = counts[0, N_GROUPS:N_GROUPS + N_EXPERTS].astype(jnp.int32)
    padded = (cnt + tm_e - 1) // tm_e * tm_e
    ends_pad = jnp.cumsum(padded)
    start_pad = ends_pad - padded
    dest = start_pad[eid] + rank
    n_blocks = (n_slot + N_EXPERTS * (tm_e - 1)) // tm_e
    n_buf = n_blocks * tm_e
    blk_e = jnp.minimum(jnp.searchsorted(ends_pad, jnp.arange(n_blocks) * tm_e, side='right'),
                        N_EXPERTS - 1).astype(jnp.int32)
    n_active = (ends_pad[-1:] // tm_e).astype(jnp.int32)
    dest_tiles = dest.reshape(rows // tm_r, tm_r, TOP_K).transpose(0, 2, 1).reshape(-1)
    xs = _moe_dispatch(dest_tiles, m, n_buf)
    yb = _moe_experts(xs, blk_e, n_active, w_in, w_out)
    return _moe_combine(dest_tiles, yb, route, h1, mod, bsz=bsz, n=n)


def _router_weights(w_group, b_group, w_expert, b_expert):
    d = w_group.shape[0]
    pad = ROUTER_LANES - N_GROUPS - N_EXPERTS
    w = jnp.concatenate([w_group, w_expert, jnp.zeros((d, pad), F32)], axis=1).astype(F32)
    b = jnp.concatenate([b_group, b_expert, jnp.zeros((pad,), F32)]).astype(F32).reshape(1, ROUTER_LANES)
    return w, b


DN_PROJ_CHUNK = 512


def _dn_proj_kernel(h_ref, mod_ref, ng_ref, w_ref, pq_ref, z_ref, ab_ref):
    a = _rms_mod(h_ref[...], ng_ref[...], mod_ref[0, 1:2, :], mod_ref[0, 0:1, :]).astype(BF16)
    qkv_w = pq_ref.shape[1]
    z_w = z_ref.shape[1]
    for c0 in range(0, qkv_w, DN_PROJ_CHUNK):
        pq_ref[:, c0:c0 + DN_PROJ_CHUNK] = _dot(a, w_ref[:, c0:c0 + DN_PROJ_CHUNK])
    for c0 in range(0, z_w, DN_PROJ_CHUNK):
        z_ref[:, c0:c0 + DN_PROJ_CHUNK] = _dot(a, w_ref[:, qkv_w + c0:qkv_w + c0 + DN_PROJ_CHUNK])
    ab_ref[...] = _dot(a, w_ref[:, qkv_w + z_w:])


def _dn_projection(h, mod, norm_gain, w_in, *, bsz, n):
    t_all, d = h.shape
    tm = ROW_TILE
    tiles_per_seq = n // tm
    qkv_w = 2 * DN_HEADS * DN_DK + DN_HEADS * DN_DV
    z_w = DN_HEADS * DN_DV
    ab_w = w_in.shape[1] - qkv_w - z_w
    w = jnp.concatenate([w_in, jnp.zeros((d, LANES - ab_w), w_in.dtype)], axis=1).astype(BF16)
    row_spec = lambda wd: pl.BlockSpec((tm, wd), lambda i: (i, 0))
    return pl.pallas_call(
        _dn_proj_kernel,
        out_shape=(jax.ShapeDtypeStruct((t_all, qkv_w), F32),
                   jax.ShapeDtypeStruct((t_all, z_w), F32),
                   jax.ShapeDtypeStruct((t_all, LANES), F32)),
        grid=(t_all // tm,),
        in_specs=[row_spec(d),
                  pl.BlockSpec((1, N_MOD, d), lambda i: (jnp.minimum(i // tiles_per_seq, bsz), 0, 0)),
                  pl.BlockSpec((1, d), lambda i: (0, 0)),
                  pl.BlockSpec(w.shape, lambda i: (0, 0))],
        out_specs=(row_spec(qkv_w), row_spec(z_w), row_spec(LANES)),
        compiler_params=_cparams("arbitrary"),
        name="dn_projection",
    )(h, mod, norm_gain.reshape(1, d), w)


def _dn_conv_kernel(pq_ref, prev_ref, next_ref, cw_ref, ab_ref, par_ref, q_ref, k_ref, v_ref, gb_ref, xe_ref,
                    *, lat_tiles, lat_tiles_per_seq, ctx_tiles_per_seq):
    i = pl.program_id(0)
    tm = pq_ref.shape[0]
    halo = SUBLANES
    pad = CONV_K // 2
    pos = jnp.where(i < lat_tiles, i % lat_tiles_per_seq, (i - lat_tiles) % ctx_tiles_per_seq)
    per_seq = jnp.where(i < lat_tiles, lat_tiles_per_seq, ctx_tiles_per_seq)
    first = pos == 0
    last = pos == per_seq - 1
    xe_ref[halo:halo + tm, :] = pq_ref[...]
    xe_ref[0:halo, :] = jnp.where(first, 0.0, prev_ref[...])
    xe_ref[halo + tm:, :] = jnp.where(last, 0.0, next_ref[...])
    n_heads = q_ref.shape[1] // LANES
    for c in range(3 * n_heads):
        cs = slice(c * LANES, (c + 1) * LANES)
        acc = jnp.zeros((tm, LANES), F32)
        for j in range(CONV_K):
            acc = acc + cw_ref[j:j + 1, cs] * xe_ref[halo - pad + j:halo - pad + j + tm, cs]
        y = _silu(acc)
        hd = c % n_heads
        hs = slice(hd * LANES, (hd + 1) * LANES)
        if c < 2 * n_heads:
            y = y * lax.rsqrt(jnp.sum(y * y, axis=-1, keepdims=True) + EPS)
        if c < n_heads:
            q_ref[:, hs] = (y * (DN_DK ** -0.5)).astype(BF16)
        elif c < 2 * n_heads:
            k_ref[:, hs] = y.astype(BF16)
        else:
            v_ref[:, hs] = y.astype(BF16)
    ab = ab_ref[...]
    xa = ab + par_ref[1:2, :]
    softplus = jnp.maximum(xa, 0.0) + jnp.log(1.0 + jnp.exp(-jnp.abs(xa)))
    g = -jnp.exp(par_ref[0:1, :]) * softplus
    beta = 1.0 / (1.0 + jnp.exp(-ab))
    r = lax.broadcasted_iota(jnp.int32, (tm, tm), 0)
    cc = lax.broadcasted_iota(jnp.int32, (tm, tm), 1)
    same = (r // DN_CHUNK) == (cc // DN_CHUNK)
    m_f = jnp.where(same & (cc <= r), 1.0, 0.0).astype(BF16)
    m_b = jnp.where(same & (cc >= r), 1.0, 0.0).astype(BF16)
    g_hi, g_lo = _split_bf16(g)
    gc_f = _dot(m_f, g_hi) + _dot(m_f, g_lo)
    gc_b = _dot(m_b, g_hi) + _dot(m_b, g_lo)
    lane = lax.broadcasted_iota(jnp.int32, (tm, LANES), 1)
    slab = jnp.where(lane < n_heads, gc_f, jnp.where(lane < 2 * n_heads, gc_b, jnp.where(lane < 4 * n_heads, beta, 0.0)))
    gb_ref[...] = slab


def _dn_conv(pq, ab, conv_w, a_log, dt_bias, *, bsz, n, l):
    t_all, qkv_w = pq.shape
    tm = DN_ROW_TILE
    halo = SUBLANES
    n_tiles = t_all // tm
    hb = tm // halo
    lat_tiles = bsz * n // tm
    hw = DN_HEADS * DN_DK
    cw = jnp.concatenate([conv_w, jnp.zeros((SUBLANES - CONV_K, qkv_w), conv_w.dtype)], axis=0).astype(F32)
    n_par = 2 * DN_HEADS
    par = jnp.stack([jnp.concatenate([a_log.reshape(-1), jnp.zeros((LANES - n_par,), F32)]),
                     jnp.concatenate([dt_bias.reshape(-1), jnp.zeros((LANES - n_par,), F32)])]).astype(F32)
    kern = functools.partial(_dn_conv_kernel, lat_tiles=lat_tiles, lat_tiles_per_seq=n // tm,
                             ctx_tiles_per_seq=l // tm)
    row_spec = lambda wd: pl.BlockSpec((tm, wd), lambda i: (i, 0))
    return pl.pallas_call(
        kern,
        out_shape=(jax.ShapeDtypeStruct((t_all, hw), BF16),
                   jax.ShapeDtypeStruct((t_all, hw), BF16),
                   jax.ShapeDtypeStruct((t_all, hw), BF16),
                   jax.ShapeDtypeStruct((t_all, LANES), F32)),
        grid=(n_tiles,),
        in_specs=[row_spec(qkv_w),
                  pl.BlockSpec((halo, qkv_w), lambda i: (jnp.maximum(i * hb - 1, 0), 0)),
                  pl.BlockSpec((halo, qkv_w), lambda i: (jnp.minimum((i + 1) * hb, n_tiles * hb - 1), 0)),
                  pl.BlockSpec((SUBLANES, qkv_w), lambda i: (0, 0)),
                  row_spec(LANES),
                  pl.BlockSpec((2, LANES), lambda i: (0, 0))],
        out_specs=(row_spec(hw), row_spec(hw), row_spec(hw), row_spec(LANES)),
        scratch_shapes=[pltpu.VMEM((tm + 2 * halo, qkv_w), F32)],
        compiler_params=_cparams("arbitrary"),
        name="dn_conv",
    )(pq, pq, pq, cw, ab, par)


def _unit_tri_inverse(lmat, r, c):
    nn = lmat.shape[0]
    inv = jnp.where(r == c, 1.0, 0.0) - jnp.where((r // 2) == (c // 2), lmat, 0.0)
    s = 2
    while s < nn:
        off = ((r // (2 * s)) == (c // (2 * s))) & ((r // s) != (c // s))
        cm = jnp.where(off, lmat, 0.0).astype(BF16)
        inv_b = inv.astype(BF16)
        inv = inv - _dot(_dot(inv_b, cm).astype(BF16), inv_b)
        s *= 2
    return inv


def _delta_chunk(q, k, v, gb, s_ref, lane_g, lane_b, lower):
    cs = q.shape[0]
    r = lax.broadcasted_iota(jnp.int32, (cs, cs), 0)
    c = lax.broadcasted_iota(jnp.int32, (cs, cs), 1)
    strict = (c < r) if lower else (c > r)
    incl = (c <= r) if lower else (c >= r)
    lane = lax.broadcasted_iota(jnp.int32, (cs, LANES), 1)
    a_col = jnp.broadcast_to(gb[:, lane_g:lane_g + 1], (cs, LANES))
    beta = jnp.broadcast_to(gb[:, lane_b:lane_b + 1], (cs, LANES))
    sel = jnp.where(lane == lane_g, 1.0, 0.0).astype(BF16)
    g_hi, g_lo = _split_bf16(gb)
    b_row = _dot_nt(sel, g_hi) + _dot_nt(sel, g_lo)
    decay = jnp.exp(jnp.where(incl, a_col[:, :cs] - b_row, -jnp.inf))
    kf = k.astype(F32)
    gp = _dot_nt(jnp.concatenate([k, q], axis=0), k)
    lmat = jnp.where(strict, beta[:, :cs] * gp[:cs] * decay, 0.0)
    qk = jnp.where(incl, gp[cs:] * decay, 0.0)
    tinv = _unit_tri_inverse(lmat, r, c)
    ea = jnp.exp(a_col)
    kb = kf * beta
    rhs = jnp.concatenate([v.astype(F32) * beta, kb * ea], axis=1)
    sol = _dot(tinv.astype(BF16), rhs.astype(BF16))
    u = sol[:, :LANES]
    w = sol[:, LANES:]
    glast = a_col[cs - 1:cs, :] if lower else a_col[0:1, :]
    kd = kf * jnp.exp(glast - a_col)
    qg = q.astype(F32) * ea
    state = s_ref[...]
    ws = _dot(jnp.concatenate([w, qg], axis=0).astype(BF16), state.astype(BF16))
    v_new = (u - ws[:cs]).astype(BF16)
    o = ws[cs:] + _dot(qk.astype(BF16), v_new)
    s_ref[...] = state * jnp.exp(glast) + lax.dot_general(
        kd.astype(BF16), v_new, (((0,), (0,)), ((), ())), preferred_element_type=F32)
    return o


def _dn_scan_kernel(qf_ref, kf_ref, vf_ref, gf_ref, qb_ref, kb_ref, vb_ref, gbw_ref, of_ref, ob_ref, s_ref):
    @pl.when(pl.program_id(1) == 0)
    def _():
        s_ref[...] = jnp.zeros_like(s_ref)

    gf = gf_ref[...]
    gbw = gbw_ref[...]
    for hd in range(DN_HEADS):
        hs = slice(hd * LANES, (hd + 1) * LANES)
        of_ref[:, hs] = _delta_chunk(qf_ref[:, hs], kf_ref[:, hs], vf_ref[:, hs], gf, s_ref.at[0, hd],
                                     hd, 2 * DN_HEADS + hd, True)
        ob_ref[:, hs] = _delta_chunk(qb_ref[:, hs], kb_ref[:, hs], vb_ref[:, hs], gbw, s_ref.at[1, hd],
                                     DN_HEADS + hd, 3 * DN_HEADS + hd, False)


def _dn_scan(q, k, v, gb, *, bsz, n, l):
    t_all, hw = q.shape
    cs = DN_CHUNK
    nc, lc = n // cs, l // cs
    ctx0 = bsz * n // cs

    def fwd_idx(b, s):
        return (jnp.where(s < lc, ctx0 + b * lc + s, b * nc + (s - lc)), 0)

    def bwd_idx(b, s):
        return (jnp.where(s < lc, ctx0 + b * lc + (lc - 1 - s), b * nc + (nc - 1 - (s - lc))), 0)

    specs = lambda idx: [pl.BlockSpec((cs, hw), idx)] * 3 + [pl.BlockSpec((cs, LANES), idx)]
    return pl.pallas_call(
        _dn_scan_kernel,
        out_shape=(jax.ShapeDtypeStruct((t_all, hw), F32), jax.ShapeDtypeStruct((t_all, hw), F32)),
        grid=(bsz, nc + lc),
        in_specs=specs(fwd_idx) + specs(bwd_idx),
        out_specs=(pl.BlockSpec((cs, hw), fwd_idx), pl.BlockSpec((cs, hw), bwd_idx)),
        scratch_shapes=[pltpu.VMEM((2, DN_HEADS, DN_DK, DN_DV), F32)],
        compiler_params=_cparams("arbitrary", "arbitrary"),
        name="dn_scan",
    )(q, k, v, gb, q, k, v, gb)


def kernel(x, c, ctx, c_ctx, ada_w, ada_b, norm_g, attn_w_qkv, attn_w_o, attn_q_gain, attn_k_gain, attn_sink, dn_w_in, dn_conv_w, dn_a_log, dn_dt_bias, dn_o_gain, dn_w_o, moe_w_group, moe_b_group, moe_w_expert, moe_b_expert, moe_w_in, moe_w_out):
    bsz, n, d = x.shape
    l = ctx.shape[1]
    depth = ada_w.shape[0]
    t_lat = bsz * n
    cond = jnp.concatenate([c.astype(F32), c_ctx.astype(F32)[None],
                            jnp.zeros((SUBLANES - (bsz + 1) % SUBLANES, d), F32)], axis=0)
    mods = _ada_modulation(cond, ada_w.astype(F32), ada_b.astype(F32))
    h = jnp.concatenate([x.reshape(t_lat, d), ctx.reshape(bsz * l, d)], axis=0)
    for i in range(depth):
        last = i == depth - 1
        j = i // 2
        mod = mods[i, :bsz + 1].reshape(bsz + 1, N_MOD, d)
        w_router, b_router = _router_weights(moe_w_group[i], moe_b_group[i], moe_w_expert[i], moe_b_expert[i])
        rows = t_lat if last else h.shape[0]
        if i % 2 == 0:
            q, kw, vw = _attn_projection(h, mod, norm_g[i, 0], attn_w_qkv[j], attn_q_gain[j], attn_k_gain[j],
                                         bsz=bsz, n=n)
            o = _attention(q, kw, vw, attn_sink[j], bsz=bsz, n=n, l=l)
            h1, m, route, counts = _mixer_output("attn", (o,), attn_w_o[j].astype(BF16), h, mod, norm_g[i, 1],
                                                 w_router, b_router, bsz=bsz, n=n, rows=rows)
        else:
            pq, z, ab = _dn_projection(h, mod, norm_g[i, 0], dn_w_in[j], bsz=bsz, n=n)
            qd, kd, vd, gb = _dn_conv(pq, ab, dn_conv_w[j], dn_a_log[j], dn_dt_bias[j], bsz=bsz, n=n, l=l)
            o_f, o_b = _dn_scan(qd, kd, vd, gb, bsz=bsz, n=n, l=l)
            og = dn_o_gain[j].astype(F32).reshape(1, DN_DV)
            h1, m, route, counts = _mixer_output("dn", (o_f, o_b, z, og), dn_w_o[j].astype(BF16), h, mod,
                                                 norm_g[i, 1], w_router, b_router, bsz=bsz, n=n, rows=rows)
        h = _moe_layer(h1, m, route, counts, mod, moe_w_in[i], moe_w_out[i], bsz=bsz, n=n)
    return h[:t_lat].reshape(bsz, n, d)
```

```python
import functools

import jax
import jax.numpy as jnp
from jax import lax
from jax.experimental import pallas as pl
from jax.experimental.pallas import tpu as pltpu

F32 = jnp.float32
BF16 = jnp.bfloat16

EPS = 1e-6
N_MOD = 6
GRID_W = 64
ATT_HEADS = 16
ATT_KV_HEADS = 4
HEAD_DIM = 64
ATT_BLOCK = 128
WINDOW = 128
ROPE_BASE = 10000.0
ROPE_FREQS = HEAD_DIM // 4
DN_HEADS = 8
DN_DK = 128
DN_DV = 128
CONV_K = 5
DN_CHUNK = 64
N_GROUPS = 4
EXPERTS_PER_GROUP = 8
N_EXPERTS = N_GROUPS * EXPERTS_PER_GROUP
TOP_K = 2
EXPERT_FF = 512

LANES = 128
SUBLANES = 8
VMEM_LIMIT_BYTES = 56 * 1024 * 1024

ROW_TILE = 512
DN_ROW_TILE = 256
MOE_TILE = 512
ROUTER_LANES = 128


def _cparams(*sem):
    return pltpu.CompilerParams(dimension_semantics=sem, vmem_limit_bytes=VMEM_LIMIT_BYTES)


def _dot(a, b):
    return jnp.dot(a, b, preferred_element_type=F32)


def _dot_nt(a, b):
    return lax.dot_general(a, b, (((1,), (1,)), ((), ())), preferred_element_type=F32)


def _split_bf16(a):
    hi = a.astype(BF16)
    lo = (a - hi.astype(F32)).astype(BF16)
    return hi, lo


def _dot3(a, b):
    ah, al = _split_bf16(a)
    bh, bl = _split_bf16(b)
    return _dot(ah, bh) + _dot(al, bh) + _dot(ah, bl)


def _silu(x):
    return x * (1.0 / (1.0 + jnp.exp(-x)))


def _rms_mod(x, gain, scale, shift):
    y = x * lax.rsqrt(jnp.mean(x * x, axis=-1, keepdims=True) + EPS) * gain
    return y * (1.0 + scale) + shift


def _ada_kernel(s_ref, w_ref, b_ref, o_ref):
    s = _silu(s_ref[...])
    o_ref[0] = _dot3(s, w_ref[0]) + b_ref[0]


def _ada_modulation(cond, ada_w, ada_b):
    depth, d, _ = ada_w.shape
    rows = cond.shape[0]
    return pl.pallas_call(
        _ada_kernel,
        out_shape=jax.ShapeDtypeStruct((depth, rows, N_MOD * d), F32),
        grid=(depth * N_MOD,),
        in_specs=[
            pl.BlockSpec((rows, d), lambda j: (0, 0)),
            pl.BlockSpec((1, d, d), lambda j: (j // N_MOD, 0, j % N_MOD)),
            pl.BlockSpec((1, 1, d), lambda j: (j // N_MOD, 0, j % N_MOD)),
        ],
        out_specs=pl.BlockSpec((1, rows, d), lambda j: (j // N_MOD, 0, j % N_MOD)),
        compiler_params=_cparams("arbitrary"),
        name="ada_modulation",
    )(cond, ada_w, ada_b.reshape(depth, 1, N_MOD * d))


def _attn_proj_kernel(h_ref, mod_ref, ng_ref, w_ref, hg_ref, gm_ref, cos_ref, sin_ref,
                      q_ref, k_ref, v_ref, *, n_q_blocks, n_kv_blocks):
    a = _rms_mod(h_ref[...], ng_ref[...], mod_ref[0, 1:2, :], mod_ref[0, 0:1, :]).astype(BF16)
    cos = cos_ref[...]
    sin = sin_ref[...]
    lane = lax.broadcasted_iota(jnp.int32, (1, LANES), 1)
    first_half = (lane % (2 * ROPE_FREQS)) < ROPE_FREQS
    for j in range(n_q_blocks + n_kv_blocks):
        is_q = j < n_q_blocks
        p = _dot(a, w_ref[:, j * LANES:(j + 1) * LANES])
        ms = _dot((p * p).astype(BF16), gm_ref[0 if is_q else 1])
        y = p * lax.rsqrt(ms + EPS) * hg_ref[(0 if is_q else 1):(1 if is_q else 2), :]
        partner = jnp.where(first_half, pltpu.roll(y, LANES - ROPE_FREQS, 1), pltpu.roll(y, ROPE_FREQS, 1))
        y = y * cos + partner * sin
        if is_q:
            q_ref[:, j * LANES:(j + 1) * LANES] = (y * (HEAD_DIM ** -0.5)).astype(BF16)
        else:
            jj = j - n_q_blocks
            k_ref[:, jj * LANES:(jj + 1) * LANES] = y.astype(BF16)
    for j in range(n_kv_blocks):
        c0 = (n_q_blocks + n_kv_blocks + j) * LANES
        v_ref[:, j * LANES:(j + 1) * LANES] = _dot(a, w_ref[:, c0:c0 + LANES]).astype(BF16)


def _rope_tables(n, pad_rows):
    rows = n // GRID_W
    row = jnp.repeat(jnp.arange(rows, dtype=F32), GRID_W)
    col = jnp.tile(jnp.arange(GRID_W, dtype=F32), rows)
    inv = ROPE_BASE ** (-jnp.arange(ROPE_FREQS, dtype=F32) / ROPE_FREQS)
    ang_r = row[:, None] * inv
    ang_c = col[:, None] * inv
    ang = jnp.concatenate([ang_r, ang_r, ang_c, ang_c], axis=1)
    sign = jnp.tile(jnp.concatenate([-jnp.ones((ROPE_FREQS,), F32), jnp.ones((ROPE_FREQS,), F32)]), 2)
    cos = jnp.tile(jnp.cos(ang), (1, 2))
    sin = jnp.tile(jnp.sin(ang) * sign, (1, 2))
    cos = jnp.concatenate([cos, jnp.ones((pad_rows, LANES), F32)], axis=0)
    sin = jnp.concatenate([sin, jnp.zeros((pad_rows, LANES), F32)], axis=0)
    return cos, sin


def _attn_projection(h, mod, norm_gain, w_qkv, q_gain, k_gain, *, bsz, n):
    t_all, d = h.shape
    tm = ROW_TILE
    tiles_per_seq = n // tm
    lat_tiles = bsz * tiles_per_seq
    q_w = ATT_HEADS * HEAD_DIM
    kv_w = ATT_KV_HEADS * HEAD_DIM
    n_q_blocks = q_w // LANES
    wq = w_qkv[:, :q_w]
    wk = w_qkv[:, q_w:q_w + kv_w].reshape(d, ATT_KV_HEADS, 1, HEAD_DIM)
    wv = w_qkv[:, q_w + kv_w:].reshape(d, ATT_KV_HEADS, 1, HEAD_DIM)
    wk = jnp.broadcast_to(wk, (d, ATT_KV_HEADS, 2, HEAD_DIM)).reshape(d, ATT_KV_HEADS * LANES)
    wv = jnp.broadcast_to(wv, (d, ATT_KV_HEADS, 2, HEAD_DIM)).reshape(d, ATT_KV_HEADS * LANES)
    w = jnp.concatenate([wq, wk, wv], axis=1).astype(BF16)
    head_gain = jnp.stack([jnp.tile(q_gain, 2), jnp.tile(k_gain, 2)]).astype(F32)
    same_head = (jnp.arange(LANES)[:, None] // HEAD_DIM) == (jnp.arange(LANES)[None, :] // HEAD_DIM)
    gm = jnp.stack([same_head.astype(F32) / HEAD_DIM, jnp.full((LANES, LANES), 1.0 / LANES, F32)]).astype(BF16)
    cos, sin = _rope_tables(n, tm)

    def rope_idx(i):
        return (jnp.where(i < lat_tiles, i % tiles_per_seq, tiles_per_seq), 0)

    kern = functools.partial(_attn_proj_kernel, n_q_blocks=n_q_blocks, n_kv_blocks=ATT_KV_HEADS)
    return pl.pallas_call(
        kern,
        out_shape=(jax.ShapeDtypeStruct((t_all, q_w), BF16),
                   jax.ShapeDtypeStruct((t_all, ATT_KV_HEADS * LANES), BF16),
                   jax.ShapeDtypeStruct((t_all, ATT_KV_HEADS * LANES), BF16)),
        grid=(t_all // tm,),
        in_specs=[
            pl.BlockSpec((tm, d), lambda i: (i, 0)),
            pl.BlockSpec((1, N_MOD, d), lambda i: (jnp.minimum(i // tiles_per_seq, bsz), 0, 0)),
            pl.BlockSpec((1, d), lambda i: (0, 0)),
            pl.BlockSpec(w.shape, lambda i: (0, 0)),
            pl.BlockSpec((2, LANES), lambda i: (0, 0)),
            pl.BlockSpec((2, LANES, LANES), lambda i: (0, 0, 0)),
            pl.BlockSpec((tm, LANES), rope_idx),
            pl.BlockSpec((tm, LANES), rope_idx),
        ],
        out_specs=(pl.BlockSpec((tm, q_w), lambda i: (i, 0)),
                   pl.BlockSpec((tm, ATT_KV_HEADS * LANES), lambda i: (i, 0)),
                   pl.BlockSpec((tm, ATT_KV_HEADS * LANES), lambda i: (i, 0))),
        compiler_params=_cparams("arbitrary"),
        name="attn_projection",
    )(h, mod, norm_gain.reshape(1, d), w, head_gain, gm, cos, sin)


def _attention_kernel(sink_ref, q_ref, kp_ref, kc_ref, kn_ref, kx_ref, vp_ref, vc_ref, vn_ref, vx_ref,
                      o_ref, *, n_lat_blocks):
    i = pl.program_id(1)
    blk = ATT_BLOCK
    rep = ATT_HEADS // ATT_KV_HEADS
    rows = rep * blk
    is_lat = i < n_lat_blocks
    r = lax.broadcasted_iota(jnp.int32, (rows, 3 * blk), 0) % blk
    c = lax.broadcasted_iota(jnp.int32, (rows, 3 * blk), 1)
    band = (c >= r) & (c <= r + 2 * WINDOW)
    lo_ok = jnp.where(i > 0, 0, blk)
    hi_ok = jnp.where(i < n_lat_blocks - 1, 3 * blk, 2 * blk)
    valid = band & (c >= lo_ok) & (c < hi_ok) & is_lat
    bias = jnp.where(valid, 0.0, -jnp.inf).astype(F32)
    lane = lax.broadcasted_iota(jnp.int32, (1, LANES), 1)
    low = lane < HEAD_DIM
    row_id = lax.broadcasted_iota(jnp.int32, (rows, 1), 0)
    for g in range(ATT_KV_HEADS):
        cols = slice(g * LANES, (g + 1) * LANES)
        parts = []
        for pb in range(rep // 2):
            qp = q_ref[:, (g * (rep // 2) + pb) * LANES:(g * (rep // 2) + pb + 1) * LANES]
            zero = jnp.zeros_like(qp)
            parts += [jnp.where(low, qp, zero), jnp.where(low, zero, qp)]
        q4 = jnp.concatenate(parts, axis=0)
        kwin = jnp.concatenate([kp_ref[:, cols], kc_ref[:, cols], kn_ref[:, cols]], axis=0)
        vwin = jnp.concatenate([vp_ref[:, cols], vc_ref[:, cols], vn_ref[:, cols]], axis=0)
        s_w = _dot_nt(q4, kwin) + bias
        s_c = _dot_nt(q4, kx_ref[:, cols])
        sk = jnp.zeros((rows, 1), F32)
        for rr in range(rep):
            sk = jnp.where(row_id // blk == rr, sink_ref[g * rep + rr], sk)
        m = jnp.maximum(jnp.maximum(jnp.max(s_w, axis=-1, keepdims=True),
                                    jnp.max(s_c, axis=-1, keepdims=True)), sk)
        p_w = jnp.exp(s_w - m)
        p_c = jnp.exp(s_c - m)
        den = jnp.sum(p_w, axis=-1, keepdims=True) + jnp.sum(p_c, axis=-1, keepdims=True) + jnp.exp(sk - m)
        o = _dot(p_w.astype(BF16), vwin) + _dot(p_c.astype(BF16), vx_ref[:, cols])
        o = o / den
        for pb in range(rep // 2):
            even = o[(2 * pb) * blk:(2 * pb + 1) * blk]
            odd = o[(2 * pb + 1) * blk:(2 * pb + 2) * blk]
            c0 = (g * (rep // 2) + pb) * LANES
            o_ref[:, c0:c0 + LANES] = jnp.where(low, even, odd).astype(o_ref.dtype)


def _attention(q, kw, vw, sink, *, bsz, n, l):
    t_all, q_w = q.shape
    blk = ATT_BLOCK
    nb = n // blk
    cb = l // blk
    lat_blocks = bsz * nb
    kvw = kw.shape[1]

    def q_idx(b, i, s):
        return (jnp.where(i < nb, b * nb + i, lat_blocks + b * cb + (i - nb)), 0)

    def win_idx(off):
        def f(b, i, s):
            j = jnp.clip(jnp.where(i < nb, i, 0) + off, 0, nb - 1)
            return (b * nb + j, 0)
        return f

    def ctx_idx(b, i, s):
        return ((bsz * n) // l + b, 0)

    kv_spec = lambda off: pl.BlockSpec((blk, kvw), win_idx(off))
    grid_spec = pltpu.PrefetchScalarGridSpec(
        num_scalar_prefetch=1,
        grid=(bsz, nb + cb),
        in_specs=[pl.BlockSpec((blk, q_w), q_idx),
                  kv_spec(-1), kv_spec(0), kv_spec(1), pl.BlockSpec((l, kvw), ctx_idx),
                  kv_spec(-1), kv_spec(0), kv_spec(1), pl.BlockSpec((l, kvw), ctx_idx)],
        out_specs=pl.BlockSpec((blk, q_w), q_idx),
    )
    return pl.pallas_call(
        functools.partial(_attention_kernel, n_lat_blocks=nb),
        out_shape=jax.ShapeDtypeStruct((t_all, q_w), BF16),
        grid_spec=grid_spec,
        compiler_params=_cparams("arbitrary", "arbitrary"),
        name="window_attention",
    )(sink.astype(F32), q, kw, kw, kw, kw, vw, vw, vw, vw)


def _route(logits, prefix_fn):
    tm = logits.shape[0]
    lane = lax.broadcasted_iota(jnp.int32, (tm, ROUTER_LANES), 1)
    neg = jnp.full_like(logits, -jnp.inf)
    big = jnp.full_like(lane, ROUTER_LANES)
    is_g = lane < N_GROUPS
    gl = jnp.where(is_g, logits, neg)
    gmax = jnp.max(gl, axis=-1, keepdims=True)
    g_idx = jnp.min(jnp.where(is_g & (gl == gmax), lane, big), axis=-1, keepdims=True)
    p_top = 1.0 / jnp.sum(jnp.exp(gl - gmax), axis=-1, keepdims=True)
    e_lo = N_GROUPS + g_idx * EXPERTS_PER_GROUP
    in_grp = (lane >= e_lo) & (lane < e_lo + EXPERTS_PER_GROUP)
    el = jnp.where(in_grp, logits, neg)
    v1 = jnp.max(el, axis=-1, keepdims=True)
    l1 = jnp.min(jnp.where(in_grp & (el == v1), lane, big), axis=-1, keepdims=True)
    el2 = jnp.where(lane == l1, neg, el)
    v2 = jnp.max(el2, axis=-1, keepdims=True)
    l2 = jnp.min(jnp.where(in_grp & (lane != l1) & (el2 == v2), lane, big), axis=-1, keepdims=True)
    ex = jnp.exp(v2 - v1)
    w1 = (1.0 / (1.0 + ex)) * p_top
    w2 = (ex / (1.0 + ex)) * p_top
    oh1 = (lane == l1)
    oh2 = (lane == l2)
    prefix = prefix_fn(oh1.astype(F32) + oh2.astype(F32))
    r1 = jnp.sum(jnp.where(oh1, prefix, 0.0), axis=-1, keepdims=True)
    r2 = jnp.sum(jnp.where(oh2, prefix, 0.0), axis=-1, keepdims=True)
    vals = [(l1 - N_GROUPS).astype(F32), (l2 - N_GROUPS).astype(F32), w1, w2, r1, r2]
    out = jnp.zeros_like(logits)
    for k, v in enumerate(vals):
        out = jnp.where(lane == k, v, out)
    return out


def _mixer_out_kernel(*refs, mode):
    if mode == "attn":
        o_ref, = refs[:1]
        rest = refs[1:]
    else:
        of_ref, ob_ref, z_ref, og_ref = refs[:4]
        rest = refs[4:]
    (wo_ref, h_ref, mod_ref, ng_ref, wr_ref, br_ref,
     h_out_ref, m_ref, route_ref, cnt_ref, carry_ref) = rest
    i = pl.program_id(0)

    @pl.when(i == 0)
    def _():
        carry_ref[...] = jnp.zeros_like(carry_ref)

    if mode == "attn":
        y_in = o_ref[...]
    else:
        parts = []
        for hd in range(DN_HEADS):
            cs = slice(hd * DN_DV, (hd + 1) * DN_DV)
            o = of_ref[:, cs] + ob_ref[:, cs]
            y = o * lax.rsqrt(jnp.mean(o * o, axis=-1, keepdims=True) + EPS) * og_ref[...] * _silu(z_ref[:, cs])
            parts.append(y.astype(BF16))
        y_in = jnp.concatenate(parts, axis=1)
    y = _dot(y_in, wo_ref[...])
    h1 = h_ref[...] + mod_ref[0, 2:3, :] * y
    h_out_ref[...] = h1
    m = _rms_mod(h1, ng_ref[...], mod_ref[0, 4:5, :], mod_ref[0, 3:4, :])
    m_ref[...] = m
    logits = _dot3(m, wr_ref[...]) + br_ref[...]
    tm = logits.shape[0]
    rr = lax.broadcasted_iota(jnp.int32, (tm, tm), 0)
    cc = lax.broadcasted_iota(jnp.int32, (tm, tm), 1)
    strict_lower = jnp.where(cc < rr, 1.0, 0.0).astype(BF16)

    def prefix_fn(cnt):
        carry = carry_ref[0:1, :]
        carry_ref[0:1, :] = carry + jnp.sum(cnt, axis=0, keepdims=True)
        return _dot(strict_lower, cnt.astype(BF16)) + carry

    route_ref[...] = _route(logits, prefix_fn)
    cnt_ref[...] = carry_ref[...]


def _mixer_output(mode, mix_inputs, w_o, h, mod, norm_gain, w_router, b_router, *, bsz, n, rows):
    d = h.shape[1]
    tm = ROW_TILE
    tiles_per_seq = n // tm
    row_spec = lambda w: pl.BlockSpec((tm, w), lambda i: (i, 0))
    const = lambda shp: pl.BlockSpec(shp, lambda i: tuple(0 for _ in shp))
    if mode == "attn":
        mix_specs = [row_spec(mix_inputs[0].shape[1])]
    else:
        of, ob, z, og = mix_inputs
        mix_specs = [row_spec(of.shape[1]), row_spec(ob.shape[1]), row_spec(z.shape[1]), const(og.shape)]
    return pl.pallas_call(
        functools.partial(_mixer_out_kernel, mode=mode),
        out_shape=(jax.ShapeDtypeStruct((rows, d), F32),
                   jax.ShapeDtypeStruct((rows, d), F32),
                   jax.ShapeDtypeStruct((rows, ROUTER_LANES), F32),
                   jax.ShapeDtypeStruct((SUBLANES, ROUTER_LANES), F32)),
        grid=(rows // tm,),
        in_specs=mix_specs + [
            const(w_o.shape),
            row_spec(d),
            pl.BlockSpec((1, N_MOD, d), lambda i: (jnp.minimum(i // tiles_per_seq, bsz), 0, 0)),
            const((1, d)),
            const(w_router.shape),
            const(b_router.shape),
        ],
        out_specs=(row_spec(d), row_spec(d), row_spec(ROUTER_LANES), const((SUBLANES, ROUTER_LANES))),
        scratch_shapes=[pltpu.VMEM((SUBLANES, ROUTER_LANES), F32)],
        compiler_params=_cparams("arbitrary"),
        name="mixer_output_" + mode,
    )(*mix_inputs, w_o, h, mod, norm_gain.reshape(1, d), w_router, b_router)


def _dispatch_kernel(dest_hbm, m_ref, xs_in, xs_hbm, idx_smem, idx_sem, row_sem, *, tm):
    del xs_in
    i = pl.program_id(0)
    cp = pltpu.make_async_copy(dest_hbm.at[pl.ds(i * (2 * tm), 2 * tm)], idx_smem, idx_sem)
    cp.start()
    cp.wait()

    def row_copy(r, dst_row):
        return pltpu.make_async_copy(m_ref.at[pl.ds(r, 1), :], xs_hbm.at[pl.ds(dst_row, 1), :], row_sem)

    def issue(r, carry):
        row_copy(r, idx_smem[r]).start()
        row_copy(r, idx_smem[tm + r]).start()
        return carry

    lax.fori_loop(0, tm, issue, 0)

    def drain(r, carry):
        row_copy(0, 0).wait()
        return carry

    lax.fori_loop(0, 2 * tm, drain, 0)


def _moe_dispatch(dest_tiles, m, n_buf):
    rows, d = m.shape
    tm = ROW_TILE
    n_tiles = rows // tm
    xs0 = jnp.zeros((n_buf, d), F32)
    return pl.pallas_call(
        functools.partial(_dispatch_kernel, tm=tm),
        out_shape=jax.ShapeDtypeStruct((n_buf, d), F32),
        grid=(n_tiles,),
        in_specs=[pl.BlockSpec(memory_space=pl.ANY), pl.BlockSpec((tm, d), lambda i: (i, 0)),
                  pl.BlockSpec(memory_space=pl.ANY)],
        out_specs=pl.BlockSpec(memory_space=pl.ANY),
        scratch_shapes=[pltpu.SMEM((2 * tm,), jnp.int32), pltpu.SemaphoreType.DMA, pltpu.SemaphoreType.DMA],
        input_output_aliases={2: 0},
        compiler_params=_cparams("arbitrary"),
        name="moe_dispatch",
    )(dest_tiles, m, xs0)


def _expert_kernel(blk_e_ref, nact_ref, x_ref, wi_ref, wo_ref, y_ref):
    del blk_e_ref
    i = pl.program_id(0)

    @pl.when(i < nact_ref[0])
    def _():
        x = x_ref[...].astype(BF16)
        hu = _dot(x, wi_ref[0].astype(BF16))
        act = _silu(hu[:, :EXPERT_FF]) * hu[:, EXPERT_FF:]
        y_ref[...] = _dot(act.astype(BF16), wo_ref[0].astype(BF16))

    @pl.when(i >= nact_ref[0])
    def _():
        y_ref[...] = jnp.zeros_like(y_ref)


def _moe_experts(xs, blk_e, n_active, w_in, w_out):
    n_buf, d = xs.shape
    tm = MOE_TILE
    n_blocks = n_buf // tm
    ff2 = w_in.shape[2]
    grid_spec = pltpu.PrefetchScalarGridSpec(
        num_scalar_prefetch=2,
        grid=(n_blocks,),
        in_specs=[pl.BlockSpec((tm, d), lambda i, be, na: (i, 0)),
                  pl.BlockSpec((1, d, ff2), lambda i, be, na: (be[i], 0, 0)),
                  pl.BlockSpec((1, ff2 // 2, d), lambda i, be, na: (be[i], 0, 0))],
        out_specs=pl.BlockSpec((tm, d), lambda i, be, na: (i, 0)),
    )
    return pl.pallas_call(
        _expert_kernel,
        out_shape=jax.ShapeDtypeStruct((n_buf, d), F32),
        grid_spec=grid_spec,
        compiler_params=_cparams("arbitrary"),
        name="moe_experts",
    )(blk_e, n_active, xs, w_in, w_out)


def _combine_kernel(dest_hbm, yb_hbm, route_ref, h_ref, mod_ref, o_ref,
                    idx_smem, y0_buf, y1_buf, idx_sem, row_sem, *, tm):
    i = pl.program_id(0)
    cp = pltpu.make_async_copy(dest_hbm.at[pl.ds(i * (2 * tm), 2 * tm)], idx_smem, idx_sem)
    cp.start()
    cp.wait()

    def row_copy(src_row, buf, r):
        return pltpu.make_async_copy(yb_hbm.at[pl.ds(src_row, 1), :], buf.at[pl.ds(r, 1), :], row_sem)

    def issue(r, carry):
        row_copy(idx_smem[r], y0_buf, r).start()
        row_copy(idx_smem[tm + r], y1_buf, r).start()
        return carry

    lax.fori_loop(0, tm, issue, 0)

    def drain(r, carry):
        row_copy(0, y0_buf, 0).wait()
        return carry

    lax.fori_loop(0, 2 * tm, drain, 0)
    route = route_ref[...]
    y = route[:, 2:3] * y0_buf[...] + route[:, 3:4] * y1_buf[...]
    o_ref[...] = h_ref[...] + mod_ref[0, 5:6, :] * y


def _moe_combine(dest_tiles, yb, route, h, mod, *, bsz, n):
    rows, d = h.shape
    tm = ROW_TILE
    tiles_per_seq = n // tm
    return pl.pallas_call(
        functools.partial(_combine_kernel, tm=tm),
        out_shape=jax.ShapeDtypeStruct((rows, d), F32),
        grid=(rows // tm,),
        in_specs=[pl.BlockSpec(memory_space=pl.ANY),
                  pl.BlockSpec(memory_space=pl.ANY),
                  pl.BlockSpec((tm, ROUTER_LANES), lambda i: (i, 0)),
                  pl.BlockSpec((tm, d), lambda i: (i, 0)),
                  pl.BlockSpec((1, N_MOD, d), lambda i: (jnp.minimum(i // tiles_per_seq, bsz), 0, 0))],
        out_specs=pl.BlockSpec((tm, d), lambda i: (i, 0)),
        scratch_shapes=[pltpu.SMEM((2 * tm,), jnp.int32), pltpu.VMEM((tm, d), F32), pltpu.VMEM((tm, d), F32),
                        pltpu.SemaphoreType.DMA, pltpu.SemaphoreType.DMA],
        compiler_params=_cparams("arbitrary"),
        name="moe_combine",
    )(dest_tiles, yb, route, h, mod)


def _moe_layer(h1, m, route, counts, mod, w_in, w_out, *, bsz, n):
    rows, d = m.shape
    tm_r = ROW_TILE
    tm_e = MOE_TILE
    n_slot = rows * TOP_K
    eid = route[:, 0:TOP_K].astype(jnp.int32)
    rank = route[:, 4:4 + TOP_K].astype(jnp.int32)
    cnt = counts[0, N_GROUPS:N_GROUPS + N_EXPERTS].astype(jnp.int32)
    padded = (cnt + tm_e - 1) // tm_e * tm_e
    ends_pad = jnp.cumsum(padded)
    start_pad = ends_pad - padded
    dest = start_pad[eid] + rank
    n_blocks = (n_slot + N_EXPERTS * (tm_e - 1)) // tm_e
    n_buf = n_blocks * tm_e
    blk_e = jnp.minimum(jnp.searchsorted(ends_pad, jnp.arange(n_blocks) * tm_e, side='right'),
                        N_EXPERTS - 1).astype(jnp.int32)
    n_active = (ends_pad[-1:] // tm_e).astype(jnp.int32)
    dest_tiles = dest.reshape(rows // tm_r, tm_r, TOP_K).transpose(0, 2, 1).reshape(-1)
    xs = _moe_dispatch(dest_tiles, m, n_buf)
    yb = _moe_experts(xs, blk_e, n_active, w_in, w_out)
    return _moe_combine(dest_tiles, yb, route, h1, mod, bsz=bsz, n=n)


def _router_weights(w_group, b_group, w_expert, b_expert):
    d = w_group.shape[0]
    pad = ROUTER_LANES - N_GROUPS - N_EXPERTS
    w = jnp.concatenate([w_group, w_expert, jnp.zeros((d, pad), F32)], axis=1).astype(F32)
    b = jnp.concatenate([b_group, b_expert, jnp.zeros((pad,), F32)]).astype(F32).reshape(1, ROUTER_LANES)
    return w, b


DN_PROJ_CHUNK = 512


def _dn_proj_kernel(h_ref, mod_ref, ng_ref, w_ref, pq_ref, z_ref, ab_ref):
    a = _rms_mod(h_ref[...], ng_ref[...], mod_ref[0, 1:2, :], mod_ref[0, 0:1, :]).astype(BF16)
    qkv_w = pq_ref.shape[1]
    z_w = z_ref.shape[1]
    for c0 in range(0, qkv_w, DN_PROJ_CHUNK):
        pq_ref[:, c0:c0 + DN_PROJ_CHUNK] = _dot(a, w_ref[:, c0:c0 + DN_PROJ_CHUNK])
    for c0 in range(0, z_w, DN_PROJ_CHUNK):
        z_ref[:, c0:c0 + DN_PROJ_CHUNK] = _dot(a, w_ref[:, qkv_w + c0:qkv_w + c0 + DN_PROJ_CHUNK])
    ab_ref[...] = _dot(a, w_ref[:, qkv_w + z_w:])


def _dn_projection(h, mod, norm_gain, w_in, *, bsz, n):
    t_all, d = h.shape
    tm = ROW_TILE
    tiles_per_seq = n // tm
    qkv_w = 2 * DN_HEADS * DN_DK + DN_HEADS * DN_DV
    z_w = DN_HEADS * DN_DV
    ab_w = w_in.shape[1] - qkv_w - z_w
    w = jnp.concatenate([w_in, jnp.zeros((d, LANES - ab_w), w_in.dtype)], axis=1).astype(BF16)
    row_spec = lambda wd: pl.BlockSpec((tm, wd), lambda i: (i, 0))
    return pl.pallas_call(
        _dn_proj_kernel,
        out_shape=(jax.ShapeDtypeStruct((t_all, qkv_w), F32),
                   jax.ShapeDtypeStruct((t_all, z_w), F32),
                   jax.ShapeDtypeStruct((t_all, LANES), F32)),
        grid=(t_all // tm,),
        in_specs=[row_spec(d),
                  pl.BlockSpec((1, N_MOD, d), lambda i: (jnp.minimum(i // tiles_per_seq, bsz), 0, 0)),
                  pl.BlockSpec((1, d), lambda i: (0, 0)),
                  pl.BlockSpec(w.shape, lambda i: (0, 0))],
        out_specs=(row_spec(qkv_w), row_spec(z_w), row_spec(LANES)),
        compiler_params=_cparams("arbitrary"),
        name="dn_projection",
    )(h, mod, norm_gain.reshape(1, d), w)


def _dn_conv_kernel(pq_ref, prev_ref, next_ref, cw_ref, ab_ref, par_ref, q_ref, k_ref, v_ref, gb_ref, xe_ref,
                    *, lat_tiles, lat_tiles_per_seq, ctx_tiles_per_seq):
    i = pl.program_id(0)
    tm = pq_ref.shape[0]
    halo = SUBLANES
    pad = CONV_K // 2
    pos = jnp.where(i < lat_tiles, i % lat_tiles_per_seq, (i - lat_tiles) % ctx_tiles_per_seq)
    per_seq = jnp.where(i < lat_tiles, lat_tiles_per_seq, ctx_tiles_per_seq)
    first = pos == 0
    last = pos == per_seq - 1
    xe_ref[halo:halo + tm, :] = pq_ref[...]
    xe_ref[0:halo, :] = jnp.where(first, 0.0, prev_ref[...])
    xe_ref[halo + tm:, :] = jnp.where(last, 0.0, next_ref[...])
    n_heads = q_ref.shape[1] // LANES
    for c in range(3 * n_heads):
        cs = slice(c * LANES, (c + 1) * LANES)
        acc = jnp.zeros((tm, LANES), F32)
        for j in range(CONV_K):
            acc = acc + cw_ref[j:j + 1, cs] * xe_ref[halo - pad + j:halo - pad + j + tm, cs]
        y = _silu(acc)
        hd = c % n_heads
        hs = slice(hd * LANES, (hd + 1) * LANES)
        if c < 2 * n_heads:
            y = y * lax.rsqrt(jnp.sum(y * y, axis=-1, keepdims=True) + EPS)
        if c < n_heads:
            q_ref[:, hs] = (y * (DN_DK ** -0.5)).astype(BF16)
        elif c < 2 * n_heads:
            k_ref[:, hs] = y.astype(BF16)
        else:
            v_ref[:, hs] = y.astype(BF16)
    ab = ab_ref[...]
    xa = ab + par_ref[1:2, :]
    softplus = jnp.maximum(xa, 0.0) + jnp.log(1.0 + jnp.exp(-jnp.abs(xa)))
    g = -jnp.exp(par_ref[0:1, :]) * softplus
    beta = 1.0 / (1.0 + jnp.exp(-ab))
    r = lax.broadcasted_iota(jnp.int32, (tm, tm), 0)
    cc = lax.broadcasted_iota(jnp.int32, (tm, tm), 1)
    same = (r // DN_CHUNK) == (cc // DN_CHUNK)
    m_f = jnp.where(same & (cc <= r), 1.0, 0.0).astype(BF16)
    m_b = jnp.where(same & (cc >= r), 1.0, 0.0).astype(BF16)
    g_hi, g_lo = _split_bf16(g)
    gc_f = _dot(m_f, g_hi) + _dot(m_f, g_lo)
    gc_b = _dot(m_b, g_hi) + _dot(m_b, g_lo)
    lane = lax.broadcasted_iota(jnp.int32, (tm, LANES), 1)
    slab = jnp.where(lane < n_heads, gc_f, jnp.where(lane < 2 * n_heads, gc_b, jnp.where(lane < 4 * n_heads, beta, 0.0)))
    gb_ref[...] = slab


def _dn_conv(pq, ab, conv_w, a_log, dt_bias, *, bsz, n, l):
    t_all, qkv_w = pq.shape
    tm = DN_ROW_TILE
    halo = SUBLANES
    n_tiles = t_all // tm
    hb = tm // halo
    lat_tiles = bsz * n // tm
    hw = DN_HEADS * DN_DK
    cw = jnp.concatenate([conv_w, jnp.zeros((SUBLANES - CONV_K, qkv_w), conv_w.dtype)], axis=0).astype(F32)
    n_par = 2 * DN_HEADS
    par = jnp.stack([jnp.concatenate([a_log.reshape(-1), jnp.zeros((LANES - n_par,), F32)]),
                     jnp.concatenate([dt_bias.reshape(-1), jnp.zeros((LANES - n_par,), F32)])]).astype(F32)
    kern = functools.partial(_dn_conv_kernel, lat_tiles=lat_tiles, lat_tiles_per_seq=n // tm,
                             ctx_tiles_per_seq=l // tm)
    row_spec = lambda wd: pl.BlockSpec((tm, wd), lambda i: (i, 0))
    return pl.pallas_call(
        kern,
        out_shape=(jax.ShapeDtypeStruct((t_all, hw), BF16),
                   jax.ShapeDtypeStruct((t_all, hw), BF16),
                   jax.ShapeDtypeStruct((t_all, hw), BF16),
                   jax.ShapeDtypeStruct((t_all, LANES), F32)),
        grid=(n_tiles,),
        in_specs=[row_spec(qkv_w),
                  pl.BlockSpec((halo, qkv_w), lambda i: (jnp.maximum(i * hb - 1, 0), 0)),
                  pl.BlockSpec((halo, qkv_w), lambda i: (jnp.minimum((i + 1) * hb, n_tiles * hb - 1), 0)),
                  pl.BlockSpec((SUBLANES, qkv_w), lambda i: (0, 0)),
                  row_spec(LANES),
                  pl.BlockSpec((2, LANES), lambda i: (0, 0))],
        out_specs=(row_spec(hw), row_spec(hw), row_spec(hw), row_spec(LANES)),
        scratch_shapes=[pltpu.VMEM((tm + 2 * halo, qkv_w), F32)],
        compiler_params=_cparams("arbitrary"),
        name="dn_conv",
    )(pq, pq, pq, cw, ab, par)


def _each(fn, *xs):
    return jnp.stack([fn(*[x[i] for x in xs]) for i in range(xs[0].shape[0])])


def _dot_tn(a, b):
    return lax.dot_general(a, b, (((0,), (0,)), ((), ())), preferred_element_type=F32)


def _unit_tri_inverse(lmat, r, c):
    nn = lmat.shape[-1]
    inv = jnp.where(r == c, 1.0, 0.0) - jnp.where((r // 2) == (c // 2), lmat, 0.0)
    s = 2
    while s < nn:
        off = ((r // (2 * s)) == (c // (2 * s))) & ((r // s) != (c // s))
        cm = jnp.where(off, lmat, 0.0).astype(BF16)
        inv_b = inv.astype(BF16)
        inv = inv - _each(_dot, _each(_dot, inv_b, cm).astype(BF16), inv_b)
        s *= 2
    return inv


def _dn_scan_kernel(qf_ref, kf_ref, vf_ref, gf_ref, qb_ref, kb_ref, vb_ref, gbw_ref, of_ref, ob_ref, s_ref):
    @pl.when(pl.program_id(1) == 0)
    def _():
        s_ref[...] = jnp.zeros_like(s_ref)

    nh = DN_HEADS
    nb = 2 * nh
    cs = qf_ref.shape[0]

    def heads(ref):
        return [ref[:, hd * LANES:(hd + 1) * LANES] for hd in range(nh)]

    q = jnp.stack(heads(qf_ref) + heads(qb_ref))
    k = jnp.stack(heads(kf_ref) + heads(kb_ref))
    v = jnp.stack(heads(vf_ref) + heads(vb_ref))
    gf = gf_ref[...]
    gw = gbw_ref[...]
    gft = gf.T
    gwt = gw.T

    def col(g, lane):
        return jnp.broadcast_to(g[:, lane:lane + 1], (cs, LANES))

    def row(gt, lane):
        return jnp.broadcast_to(gt[lane:lane + 1, :], (cs, cs))

    a_col = jnp.stack([col(gf, hd) for hd in range(nh)] + [col(gw, nh + hd) for hd in range(nh)])
    beta = jnp.stack([col(gf, 2 * nh + hd) for hd in range(nh)] + [col(gw, 3 * nh + hd) for hd in range(nh)])
    b_row = jnp.stack([row(gft, hd) for hd in range(nh)] + [row(gwt, nh + hd) for hd in range(nh)])
    pb = lax.broadcasted_iota(jnp.int32, (nb, cs, cs), 0)
    r = lax.broadcasted_iota(jnp.int32, (nb, cs, cs), 1)
    c = lax.broadcasted_iota(jnp.int32, (nb, cs, cs), 2)
    tri = (r - c) * jnp.where(pb < nh, 1, -1)
    strict = tri > 0
    incl = tri >= 0
    decay = jnp.exp(jnp.where(incl, a_col[:, :, :cs] - b_row, -jnp.inf))
    kf = k.astype(F32)
    gp = _each(_dot_nt, jnp.concatenate([k, q], axis=1), k)
    lmat = jnp.where(strict, beta[:, :, :cs] * gp[:, :cs] * decay, 0.0)
    qk = jnp.where(incl, gp[:, cs:] * decay, 0.0)
    tinv = _unit_tri_inverse(lmat, r, c)
    ea = jnp.exp(a_col)
    kb = kf * beta
    rhs = jnp.concatenate([v.astype(F32) * beta, kb * ea], axis=2)
    sol = _each(_dot, tinv.astype(BF16), rhs.astype(BF16))
    u = sol[:, :, :LANES]
    w = sol[:, :, LANES:]
    fwd_row = lax.broadcasted_iota(jnp.int32, (nb, 1, LANES), 0) < nh
    glast = jnp.where(fwd_row, a_col[:, cs - 1:cs, :], a_col[:, 0:1, :])
    kd = kf * jnp.exp(glast - a_col)
    qg = q.astype(F32) * ea
    state = s_ref[...]
    ws = _each(_dot, jnp.concatenate([w, qg], axis=1).astype(BF16), state.astype(BF16))
    v_new = (u - ws[:, :cs]).astype(BF16)
    o = ws[:, cs:] + _each(_dot, qk.astype(BF16), v_new)
    s_ref[...] = state * jnp.exp(glast) + _each(_dot_tn, kd.astype(BF16), v_new)
    for hd in range(nh):
        of_ref[:, hd * LANES:(hd + 1) * LANES] = o[hd]
        ob_ref[:, hd * LANES:(hd + 1) * LANES] = o[nh + hd]


def _dn_scan(q, k, v, gb, *, bsz, n, l):
    t_all, hw = q.shape
    cs = DN_CHUNK
    nc, lc = n // cs, l // cs
    ctx0 = bsz * n // cs

    def fwd_idx(b, s):
        return (jnp.where(s < lc, ctx0 + b * lc + s, b * nc + (s - lc)), 0)

    def bwd_idx(b, s):
        return (jnp.where(s < lc, ctx0 + b * lc + (lc - 1 - s), b * nc + (nc - 1 - (s - lc))), 0)

    specs = lambda idx: [pl.BlockSpec((cs, hw), idx)] * 3 + [pl.BlockSpec((cs, LANES), idx)]
    return pl.pallas_call(
        _dn_scan_kernel,
        out_shape=(jax.ShapeDtypeStruct((t_all, hw), F32), jax.ShapeDtypeStruct((t_all, hw), F32)),
        grid=(bsz, nc + lc),
        in_specs=specs(fwd_idx) + specs(bwd_idx),
        out_specs=(pl.BlockSpec((cs, hw), fwd_idx), pl.BlockSpec((cs, hw), bwd_idx)),
        scratch_shapes=[pltpu.VMEM((2 * DN_HEADS, DN_DK, DN_DV), F32)],
        compiler_params=_cparams("arbitrary", "arbitrary"),
        name="dn_scan",
    )(q, k, v, gb, q, k, v, gb)


def kernel(x, c, ctx, c_ctx, ada_w, ada_b, norm_g, attn_w_qkv, attn_w_o, attn_q_gain, attn_k_gain, attn_sink, dn_w_in, dn_conv_w, dn_a_log, dn_dt_bias, dn_o_gain, dn_w_o, moe_w_group, moe_b_group, moe_w_expert, moe_b_expert, moe_w_in, moe_w_out):
    bsz, n, d = x.shape
    l = ctx.shape[1]
    depth = ada_w.shape[0]
    t_lat = bsz * n
    cond = jnp.concatenate([c.astype(F32), c_ctx.astype(F32)[None],
                            jnp.zeros((SUBLANES - (bsz + 1) % SUBLANES, d), F32)], axis=0)
    mods = _ada_modulation(cond, ada_w.astype(F32), ada_b.astype(F32))
    h = jnp.concatenate([x.reshape(t_lat, d), ctx.reshape(bsz * l, d)], axis=0)
    for i in range(depth):
        last = i == depth - 1
        j = i // 2
        mod = mods[i, :bsz + 1].reshape(bsz + 1, N_MOD, d)
        w_router, b_router = _router_weights(moe_w_group[i], moe_b_group[i], moe_w_expert[i], moe_b_expert[i])
        rows = t_lat if last else h.shape[0]
        if i % 2 == 0:
            q, kw, vw = _attn_projection(h, mod, norm_g[i, 0], attn_w_qkv[j], attn_q_gain[j], attn_k_gain[j],
                                         bsz=bsz, n=n)
            o = _attention(q, kw, vw, attn_sink[j], bsz=bsz, n=n, l=l)
            h1, m, route, counts = _mixer_output("attn", (o,), attn_w_o[j].astype(BF16), h, mod, norm_g[i, 1],
                                                 w_router, b_router, bsz=bsz, n=n, rows=rows)
        else:
            pq, z, ab = _dn_projection(h, mod, norm_g[i, 0], dn_w_in[j], bsz=bsz, n=n)
            qd, kd, vd, gb = _dn_conv(pq, ab, dn_conv_w[j], dn_a_log[j], dn_dt_bias[j], bsz=bsz, n=n, l=l)
            o_f, o_b = _dn_scan(qd, kd, vd, gb, bsz=bsz, n=n, l=l)
            og = dn_o_gain[j].astype(F32).reshape(1, DN_DV)
            h1, m, route, counts = _mixer_output("dn", (o_f, o_b, z, og), dn_w_o[j].astype(BF16), h, mod,
                                                 norm_g[i, 1], w_router, b_router, bsz=bsz, n=n, rows=rows)
        h = _moe_layer(h1, m, route, counts, mod, moe_w_in[i], moe_w_out[i], bsz=bsz, n=n)
    return h[:t_lat].reshape(bsz, n, d)
```
